```python
import math
import numpy as np
import jax
import jax.numpy as jnp
from jax import lax

D_MODEL = 2048
BATCH = 1
SEQ = 16384
DEPTH = 2

N_META = 16
GRID_W = 64
C_HY = 1024
HY_ORDER = 2
HY_EMB = 33
HY_HID = 64
SHORT_K = 3
C_FN = 1024
FN_GROUPS = 4
C_NA = 1024
NA_HEADS = 16
NA_HEAD_DIM = C_NA // NA_HEADS
NA_KH_MAX = 8
NA_KW = 16
NA_QB = 16
NA_KU = 32
N_BRANCH = 3
EPS = 1e-6
NEG_INF = -1e30
SPLIT_SIZES = ((HY_ORDER + 1) * C_HY, C_HY, C_FN, C_FN, 3 * C_NA, C_NA, N_BRANCH * D_MODEL)
N_IN = sum(SPLIT_SIZES)

kernel_name = 'hybrid_hyena_fnet_natten_encoder'


def rmsnorm(x, g):
    xf = x.astype(jnp.float32)
    y = xf * lax.rsqrt(jnp.mean(xf * xf, axis=-1, keepdims=True) + EPS)
    return (y * g.astype(jnp.float32)).astype(x.dtype)


def short_conv(x, w, b):
    L = x.shape[1]
    p = SHORT_K // 2
    xp = jnp.pad(x, ((0, 0), (p, p), (0, 0)))
    y = xp[:, 0:L] * w[0]
    for j in range(1, SHORT_K):
        y = y + xp[:, j:j + L] * w[j]
    return y + b


def hyena_filters(L, w1, b1, w2, b2, w3, b3, w4, freq, decay):
    f32 = jnp.float32
    t = jnp.linspace(0.0, 1.0, L, dtype=f32)[:, None]
    bands = (HY_EMB - 1) // 2
    w = 2.0 * math.pi * jnp.arange(L, dtype=f32)[:, None] / L
    f = jnp.linspace(1e-4, bands - 1, bands, dtype=f32)[None, :]
    z = jnp.concatenate([t, jnp.cos(f * w), -jnp.sin(f * w)], axis=-1)
    fr = freq.astype(f32)
    h = jnp.sin(fr * (z @ w1.astype(f32) + b1.astype(f32)))
    h = jnp.sin(fr * (h @ w2.astype(f32) + b2.astype(f32)))
    h = jnp.sin(fr * (h @ w3.astype(f32) + b3.astype(f32)))
    h = (h @ w4.astype(f32)).reshape(L, HY_ORDER, 2, C_HY)
    h = h * jnp.exp(-t[:, :, None, None] * jnp.abs(decay.astype(f32)))
    hf = h[:, :, 0]
    hb = h[:, :, 1]
    k = jnp.concatenate([hf, jnp.zeros((1, HY_ORDER, C_HY), f32), hb[1:][::-1]], axis=0)
    return jnp.fft.rfft(k, axis=0)


def fftconv(v, kf, skip):
    L = v.shape[1]
    vf = jnp.fft.rfft(v.astype(jnp.float32), n=2 * L, axis=1)
    y = jnp.fft.irfft(vf * kf[None], n=2 * L, axis=1)[:, :L]
    return y + v.astype(jnp.float32) * skip.astype(jnp.float32)


def hyena_mixer(u, x1, x2, kf, skip):
    z = fftconv(u, kf[:, 0], skip[0])
    z = x1.astype(jnp.float32) * z
    z = fftconv(z, kf[:, 1], skip[1])
    z = x2.astype(jnp.float32) * z
    return z.astype(u.dtype)


def fourier_mixer(xb):
    B, L, C = xb.shape
    xg = xb.astype(jnp.float32).reshape(B, L, FN_GROUPS, C // FN_GROUPS)
    y = jnp.fft.fft2(xg, axes=(1, 3), norm='ortho').real
    return y.reshape(B, L, C).astype(xb.dtype)


def na_tables(rows):
    kh = min(NA_KH_MAX, rows)
    rs = np.clip(np.arange(rows) - kh // 2, 0, rows - kh)
    cs = np.clip(np.arange(GRID_W) - NA_KW // 2, 0, GRID_W - NA_KW)
    ncb = GRID_W // NA_QB
    cu = np.clip(np.arange(ncb) * NA_QB - NA_KW // 2, 0, GRID_W - NA_KU)
    col_idx = cu[:, None] + np.arange(NA_KU)[None, :]
    qcol = np.arange(ncb)[:, None] * NA_QB + np.arange(NA_QB)[None, :]
    kcol = col_idx[:, None, :]
    qcs = cs[qcol][:, :, None]
    mask = (kcol >= qcs) & (kcol < qcs + NA_KW)
    dc_idx = np.clip(kcol - qcol[:, :, None] + NA_KW - 1, 0, 2 * NA_KW - 2)
    return kh, rs, col_idx, mask, dc_idx


def neighborhood_attention(q, k, v, rpb, meta_bias):
    B, Lt, _ = q.shape
    n = Lt - N_META
    rows = n // GRID_W
    kh, rs, col_idx, mask, dc_idx = na_tables(rows)
    H, dh = NA_HEADS, NA_HEAD_DIM
    scale = dh ** -0.5
    f32 = jnp.float32

    def heads(t):
        return t.reshape(B, Lt, H, dh).transpose(0, 2, 1, 3)

    qh, kh_, vh = heads(q), heads(k), heads(v)
    q_m, k_m, v_m = qh[:, :, :N_META], kh_[:, :, :N_META], vh[:, :, :N_META]

    def grid(t):
        return t[:, :, N_META:].reshape(B, H, rows, GRID_W, dh)

    q_g, k_g, v_g = grid(qh), grid(kh_), grid(vh)
    mb = meta_bias.astype(f32)
    rpb32 = rpb.astype(f32)

    s_m = jnp.einsum('bhqd,bhmd->bhqm', q_m, k_m).astype(f32) * scale + mb[None, :, None, :]
    o_m = jnp.einsum('bhqm,bhmd->bhqd', jax.nn.softmax(s_m, axis=-1).astype(v.dtype), v_m)

    ncb = GRID_W // NA_QB

    def row(args):
        q_r, r, r0 = args
        kb = lax.dynamic_slice_in_dim(k_g, r0, kh, axis=2)[:, :, :, col_idx]
        vb = lax.dynamic_slice_in_dim(v_g, r0, kh, axis=2)[:, :, :, col_idx]
        qb = q_r.reshape(B, H, ncb, NA_QB, dh)
        s_g = jnp.einsum('bhnqd,bhrnkd->bhnqrk', qb, kb).astype(f32) * scale
        dr = r0 + jnp.arange(kh, dtype=jnp.int32) - r + (NA_KH_MAX - 1)
        bias = jnp.take(rpb32, dr, axis=1)[:, :, dc_idx]
        s_g = s_g + bias.transpose(0, 2, 3, 1, 4)[None]
        s_g = jnp.where(mask[:, :, None, :], s_g, NEG_INF)
        s_x = jnp.einsum('bhnqd,bhmd->bhnqm', qb, k_m).astype(f32) * scale + mb[None, :, None, None, :]
        s = jnp.concatenate([s_g.reshape(B, H, ncb, NA_QB, kh * NA_KU), s_x], axis=-1)
        p = jax.nn.softmax(s, axis=-1).astype(v.dtype)
        p_g = p[..., :kh * NA_KU].reshape(B, H, ncb, NA_QB, kh, NA_KU)
        p_x = p[..., kh * NA_KU:]
        o = jnp.einsum('bhnqrk,bhrnkd->bhnqd', p_g, vb) + jnp.einsum('bhnqm,bhmd->bhnqd', p_x, v_m)
        return o.reshape(B, H, GRID_W, dh)

    o_rows = lax.map(row, (jnp.moveaxis(q_g, 2, 0), jnp.arange(rows, dtype=jnp.int32),
                           jnp.asarray(rs, dtype=jnp.int32)))
    o_g = jnp.moveaxis(o_rows, 0, 2).reshape(B, H, n, dh)
    o = jnp.concatenate([o_m, o_g], axis=2)
    return o.transpose(0, 2, 1, 3).reshape(B, Lt, C_NA)


def mixer_layer(h, norm_g, w_in, conv_w, conv_b, w1, b1, w2, b2, w3, b3, w4, freq, decay, skip,
                rpb, meta_bias, w_a, w_b, w_c, w_out):
    L = h.shape[1]
    xn = rmsnorm(h, norm_g)
    z = xn @ w_in
    idx = [int(i) for i in np.cumsum(SPLIT_SIZES)[:-1]]
    hy_in, hy_gate, fn_in, fn_gate, qkv, na_gate, merge = jnp.split(z, idx, axis=-1)
    hy_in = short_conv(hy_in, conv_w, conv_b)
    u, x1, x2 = jnp.split(hy_in, HY_ORDER + 1, axis=-1)
    kf = hyena_filters(L, w1, b1, w2, b2, w3, b3, w4, freq, decay)
    y_a = hyena_mixer(u, x1, x2, kf, skip) * jax.nn.silu(hy_gate)
    y_b = fourier_mixer(fn_in) * jax.nn.silu(fn_gate)
    q, k, v = jnp.split(qkv, 3, axis=-1)
    y_c = neighborhood_attention(q, k, v, rpb, meta_bias) * jax.nn.silu(na_gate)
    g_a, g_b, g_c = jnp.split(jax.nn.sigmoid(merge), N_BRANCH, axis=-1)
    m = g_a * (y_a @ w_a) + g_b * (y_b @ w_b) + g_c * (y_c @ w_c)
    return h + m @ w_out


def setup_inputs(seed: int = 0) -> dict:
    key = jax.random.key(seed)
    ks = jax.random.split(key, 24)

    def nrm(k, shape, std):
        return std * jax.random.normal(k, shape, jnp.float32)

    L = SEQ + N_META
    decay0 = jnp.linspace(math.log(1e-2) / 1.5, math.log(1e-2) / 0.3, C_HY, dtype=jnp.float32)
    return {
        'x': nrm(ks[0], (BATCH, SEQ, D_MODEL), 1.0),
        'meta_tokens': nrm(ks[1], (N_META, D_MODEL), 1.0),
        'norm_g': 1.0 + nrm(ks[2], (DEPTH, D_MODEL), 0.02),
        'w_in': nrm(ks[3], (DEPTH, D_MODEL, N_IN), D_MODEL ** -0.5),
        'hy_conv_w': nrm(ks[4], (DEPTH, SHORT_K, (HY_ORDER + 1) * C_HY), SHORT_K ** -0.5),
        'hy_conv_b': nrm(ks[5], (DEPTH, (HY_ORDER + 1) * C_HY), 0.02),
        'hy_flt_w1': nrm(ks[6], (DEPTH, HY_EMB, HY_HID), HY_EMB ** -0.5),
        'hy_flt_b1': nrm(ks[7], (DEPTH, HY_HID), 0.1),
        'hy_flt_w2': nrm(ks[8], (DEPTH, HY_HID, HY_HID), HY_HID ** -0.5),
        'hy_flt_b2': nrm(ks[9], (DEPTH, HY_HID), 0.1),
        'hy_flt_w3': nrm(ks[10], (DEPTH, HY_HID, HY_HID), HY_HID ** -0.5),
        'hy_flt_b3': nrm(ks[11], (DEPTH, HY_HID), 0.1),
        'hy_flt_w4': nrm(ks[12], (DEPTH, HY_HID, HY_ORDER * 2 * C_HY), 0.5 * L ** -0.5),
        'hy_flt_freq': 1.0 + nrm(ks[13], (DEPTH, HY_HID), 0.1),
        'hy_decay': jnp.broadcast_to(decay0, (DEPTH, HY_ORDER, 2, C_HY)) + nrm(ks[14], (DEPTH, HY_ORDER, 2, C_HY), 0.1),
        'hy_skip': nrm(ks[15], (DEPTH, HY_ORDER, C_HY), 0.5),
        'na_rpb': nrm(ks[16], (DEPTH, NA_HEADS, 2 * NA_KH_MAX - 1, 2 * NA_KW - 1), 0.1),
        'na_meta_bias': nrm(ks[17], (DEPTH, NA_HEADS, N_META), 0.1),
        'w_branch_a': nrm(ks[18], (DEPTH, C_HY, D_MODEL), C_HY ** -0.5),
        'w_branch_b': nrm(ks[19], (DEPTH, C_FN, D_MODEL), C_FN ** -0.5),
        'w_branch_c': nrm(ks[20], (DEPTH, C_NA, D_MODEL), C_NA ** -0.5),
        'w_out': nrm(ks[21], (DEPTH, D_MODEL, D_MODEL), D_MODEL ** -0.5),
        'final_g': 1.0 + nrm(ks[22], (D_MODEL,), 0.02),
    }


def reference(x, meta_tokens, norm_g, w_in, hy_conv_w, hy_conv_b, hy_flt_w1, hy_flt_b1, hy_flt_w2,
              hy_flt_b2, hy_flt_w3, hy_flt_b3, hy_flt_w4, hy_flt_freq, hy_decay, hy_skip, na_rpb,
              na_meta_bias, w_branch_a, w_branch_b, w_branch_c, w_out, final_g):
    B = x.shape[0]
    meta = jnp.broadcast_to(meta_tokens[None].astype(x.dtype), (B, N_META, D_MODEL))
    h = jnp.concatenate([meta, x], axis=1)
    for i in range(DEPTH):
        h = mixer_layer(h, norm_g[i], w_in[i], hy_conv_w[i], hy_conv_b[i],
                        hy_flt_w1[i], hy_flt_b1[i], hy_flt_w2[i], hy_flt_b2[i], hy_flt_w3[i], hy_flt_b3[i],
                        hy_flt_w4[i], hy_flt_freq[i], hy_decay[i], hy_skip[i],
                        na_rpb[i], na_meta_bias[i], w_branch_a[i], w_branch_b[i], w_branch_c[i], w_out[i])
    return rmsnorm(h, final_g)[:, N_META:]
```

```python
import functools
import math

import numpy as np
import jax
import jax.numpy as jnp
from jax import lax
from jax.experimental import pallas as pl
from jax.experimental.pallas import tpu as pltpu

F32 = jnp.float32
BF16 = jnp.bfloat16

D_MODEL = 2048
SEQ = 16384
N_META = 16
SEQ_T = SEQ + N_META
GRID_W = 64
GRID_H = SEQ // GRID_W
C_BR = 1024
HY_EMB = 33
HY_HID = 64
NA_HEADS = 16
NA_DH = 64
NA_KH = 8
NA_KW = 16
EPS = 1e-6
NEG_INF = -1e30

OFF_HY_IN = 0
OFF_HY_GATE = 3072
OFF_FN_IN = 4096
OFF_FN_GATE = 5120
OFF_Q = 6144
OFF_K = 7168
OFF_V = 8192
OFF_NA_GATE = 9216
OFF_MERGE = 10240
N_IN = 16384

FA = 80
FB = 205
FBP = 208
BBLK = 8
KA_HY = FA + 1
KAP_HY = 96

ROW_TILE = 656
LANE = 128
VMEM_LIMIT = 48 * 1024 * 1024


def _cparams(sem):
    return pltpu.CompilerParams(dimension_semantics=sem, vmem_limit_bytes=VMEM_LIMIT)


@functools.lru_cache(maxsize=None)
def _hyena_tables():
    n_circ = 2 * SEQ_T
    a = np.arange(FA)[None, None, :]
    b = np.arange(FBP)[:, None, None]
    ka = np.arange(KAP_HY)[None, :, None]
    n = FB * a + b
    ang = 2.0 * np.pi * ((ka * n) % n_circ) / n_circ
    valid = (b < FB) & (ka < KA_HY)
    fwd_r = np.where(valid, np.cos(ang), 0.0)
    fwd_i = np.where(valid, -np.sin(ang), 0.0)
    c = np.where((ka == 0) | (ka == FA), 1.0, 2.0) / n_circ
    inv_r = np.transpose(np.where(valid, c * np.cos(ang), 0.0), (0, 2, 1))
    inv_i = np.transpose(np.where(valid, -c * np.sin(ang), 0.0), (0, 2, 1))
    kb = np.arange(FBP)[:, None]
    bb = np.arange(FBP)[None, :]
    phi = 2.0 * np.pi * ((kb * bb) % FB) / FB
    ok = (kb < FB) & (bb < FB)
    cb = np.where(ok, np.cos(phi), 0.0)
    sb = np.where(ok, np.sin(phi), 0.0)
    return dict(fwd_r=fwd_r, fwd_i=fwd_i, inv_r=inv_r, inv_i=inv_i, cb=cb, sb=sb)


@functools.lru_cache(maxsize=None)
def _fnet_tables():
    a = np.arange(FA)[None, None, :]
    b = np.arange(FBP)[:, None, None]
    ka = np.arange(FA)[None, :, None]
    n = FB * a + b
    ang = 2.0 * np.pi * ((ka * n) % SEQ_T) / SEQ_T
    valid = np.broadcast_to(b < FB, ang.shape)
    cs = np.where(valid, np.cos(ang), 0.0)
    sn = np.where(valid, np.sin(ang), 0.0)
    kb = np.arange(FBP)[:, None]
    bb = np.arange(FBP)[None, :]
    phi = 2.0 * np.pi * ((kb * bb) % FB) / FB
    ok = (kb < FB) & (bb < FB)
    scale = 1.0 / math.sqrt(SEQ_T * 256.0)
    cb = np.where(ok, np.cos(phi), 0.0) * scale
    sb = np.where(ok, np.sin(phi), 0.0) * scale
    j = np.arange(256)
    th = 2.0 * np.pi * ((j[:, None] * j[None, :]) % 256) / 256.0
    chan = np.concatenate([np.cos(th), np.sin(th)], axis=1)
    return dict(cs=cs, sn=sn, cb=cb, sb=sb, chan=chan)


@functools.lru_cache(maxsize=None)
def _filter_features():
    t = np.linspace(0.0, 1.0, SEQ_T)[:, None]
    bands = (HY_EMB - 1) // 2
    w = 2.0 * np.pi * np.arange(SEQ_T)[:, None] / SEQ_T
    f = np.linspace(1e-4, bands - 1, bands)[None, :]
    z = np.concatenate([t, np.cos(f * w), -np.sin(f * w)], axis=-1)
    out = np.zeros((SEQ_T, HY_HID), np.float64)
    out[:, :HY_EMB] = z
    return out


@functools.lru_cache(maxsize=None)
def _na_index_tables():
    qc = np.arange(GRID_W)[:, None]
    kc = np.arange(GRID_W)[None, :]
    cs = np.clip(qc - NA_KW // 2, 0, GRID_W - NA_KW)
    colmask = (kc >= cs) & (kc < cs + NA_KW)
    dc = np.clip(kc - qc + NA_KW - 1, 0, 2 * NA_KW - 2)
    return colmask, dc


def _bf(x):
    return jnp.asarray(np.asarray(x, np.float32), dtype=BF16)


def _inproj_body(x_ref, g_ref, w_ref, o_ref, xn_ref):
    @pl.when(pl.program_id(1) == 0)
    def _():
        x = x_ref[...]
        y = x * lax.rsqrt(jnp.mean(x * x, axis=-1, keepdims=True) + EPS)
        xn_ref[...] = (y * g_ref[...]).astype(BF16)

    o_ref[...] = jnp.dot(xn_ref[...], w_ref[...], preferred_element_type=F32)


def _inproj(h, g, w_bf16, tn=1024):
    n_rows, d = h.shape
    n_out = w_bf16.shape[1]
    return pl.pallas_call(
        _inproj_body,
        grid=(n_rows // ROW_TILE, n_out // tn),
        in_specs=[
            pl.BlockSpec((ROW_TILE, d), lambda i, j: (i, 0)),
            pl.BlockSpec((1, d), lambda i, j: (0, 0)),
            pl.BlockSpec((d, tn), lambda i, j: (0, j)),
        ],
        out_specs=pl.BlockSpec((ROW_TILE, tn), lambda i, j: (i, j)),
        out_shape=jax.ShapeDtypeStruct((n_rows, n_out), F32),
        scratch_shapes=[pltpu.VMEM((ROW_TILE, d), BF16)],
        compiler_params=_cparams(("parallel", "arbitrary")),
        name="inproj",
    )(h, g.reshape(1, d), w_bf16)


def _shortconv_body(prev_ref, cur_ref, next_ref, w_ref, b_ref, o_ref):
    i = pl.program_id(0)
    last = pl.num_programs(0) - 1
    x = cur_ref[...]
    rows = lax.broadcasted_iota(jnp.int32, x.shape, 0)
    prev_row = jnp.where(i == 0, 0.0, prev_ref[7:8, :])
    next_row = jnp.where(i == last, 0.0, next_ref[0:1, :])
    up = jnp.where(rows == 0, prev_row, pltpu.roll(x, 1, 0))
    dn = jnp.where(rows == ROW_TILE - 1, next_row, pltpu.roll(x, ROW_TILE - 1, 0))
    y = up * w_ref[0:1, :]
    y = y + x * w_ref[1:2, :]
    y = y + dn * w_ref[2:3, :]
    o_ref[...] = y + b_ref[...]


def _shortconv(z, w, b, tc=1024):
    n_rows = z.shape[0]
    n_c = w.shape[1]
    n8 = n_rows // 8
    r8 = ROW_TILE // 8
    return pl.pallas_call(
        _shortconv_body,
        grid=(n_rows // ROW_TILE, n_c // tc),
        in_specs=[
            pl.BlockSpec((8, tc), lambda i, j: (jnp.maximum(i * r8 - 1, 0), j)),
            pl.BlockSpec((ROW_TILE, tc), lambda i, j: (i, j)),
            pl.BlockSpec((8, tc), lambda i, j: (jnp.minimum((i + 1) * r8, n8 - 1), j)),
            pl.BlockSpec((3, tc), lambda i, j: (0, j)),
            pl.BlockSpec((1, tc), lambda i, j: (0, j)),
        ],
        out_specs=pl.BlockSpec((ROW_TILE, tc), lambda i, j: (i, j)),
        out_shape=jax.ShapeDtypeStruct((n_rows, n_c), F32),
        compiler_params=_cparams(("parallel", "parallel")),
        name="shortconv",
    )(z, z, z, w, b.reshape(1, n_c))


def _hp_dot(a, b):
    return jnp.dot(a, b, preferred_element_type=F32, precision=lax.Precision.HIGHEST)


def _filter_body(z_ref, w1_ref, b1_ref, w2_ref, b2_ref, w3_ref, b3_ref, fr_ref, w4_ref, dec_ref, o_ref, h_ref):
    i = pl.program_id(0)
    j = pl.program_id(1)

    @pl.when(j == 0)
    def _():
        fr = fr_ref[...]
        h = jnp.sin(fr * (_hp_dot(z_ref[...], w1_ref[...]) + b1_ref[...]))
        h = jnp.sin(fr * (_hp_dot(h, w2_ref[...]) + b2_ref[...]))
        h_ref[...] = jnp.sin(fr * (_hp_dot(h, w3_ref[...]) + b3_ref[...]))

    t = z_ref[:, 0:1]
    y = _hp_dot(h_ref[...], w4_ref[...]) * jnp.exp(-t * jnp.abs(dec_ref[...]))
    rows = lax.broadcasted_iota(jnp.int32, y.shape, 0)
    drop = jnp.logical_and(jnp.logical_and(i == 0, j % 2 == 1), rows == 0)
    o_ref[...] = jnp.where(drop, 0.0, y)


def _hyena_filters(w1, b1, w2, b2, w3, b3, w4, freq, decay):
    zfeat = jnp.asarray(_filter_features(), dtype=F32)
    w1p = jnp.zeros((HY_HID, HY_HID), F32).at[:HY_EMB].set(w1)
    n_c = w4.shape[1]
    tc = C_BR
    small = lambda i, j: (0, 0)
    return pl.pallas_call(
        _filter_body,
        grid=(SEQ_T // ROW_TILE, n_c // tc),
        in_specs=[
            pl.BlockSpec((ROW_TILE, HY_HID), lambda i, j: (i, 0)),
            pl.BlockSpec((HY_HID, HY_HID), small),
            pl.BlockSpec((1, HY_HID), small),
            pl.BlockSpec((HY_HID, HY_HID), small),
            pl.BlockSpec((1, HY_HID), small),
            pl.BlockSpec((HY_HID, HY_HID), small),
            pl.BlockSpec((1, HY_HID), small),
            pl.BlockSpec((1, HY_HID), small),
            pl.BlockSpec((HY_HID, tc), lambda i, j: (0, j)),
            pl.BlockSpec((1, tc), lambda i, j: (0, j)),
        ],
        out_specs=pl.BlockSpec((ROW_TILE, tc), lambda i, j: (i, j)),
        out_shape=jax.ShapeDtypeStruct((SEQ_T, n_c), F32),
        scratch_shapes=[pltpu.VMEM((ROW_TILE, HY_HID), F32)],
        compiler_params=_cparams(("parallel", "arbitrary")),
        name="hyena_filter",
    )(zfeat, w1p, b1.reshape(1, -1), w2, b2.reshape(1, -1), w3, b3.reshape(1, -1), freq.reshape(1, -1),
      w4, decay.reshape(1, n_c))


def _stage_a_body(n_in, n_out, *refs):
    x_refs = refs[:n_in]
    w_refs = refs[n_in:n_in + n_in * n_out]
    o_refs = refs[n_in + n_in * n_out:]
    bb = pl.program_id(0)
    for j in range(BBLK):
        valid = bb * BBLK + j < FB
        xs = [jnp.where(valid, xr[:, j, :], 0.0).astype(BF16) for xr in x_refs]
        for o in range(n_out):
            acc = None
            for k in range(n_in):
                d = jnp.dot(w_refs[o * n_in + k][j], xs[k], preferred_element_type=F32)
                acc = d if acc is None else acc + d
            o_refs[o][j] = acc.astype(BF16)


def _stage_a(xs, col_offs, n_cols, w_tabs, ct=1024):
    n_in = len(xs)
    n_out = len(w_tabs)
    m = w_tabs[0][0].shape[1]
    x3 = [x.reshape(FA, FB, x.shape[1]) for x in xs]
    in_specs = []
    for k in range(n_in):
        off = col_offs[k] // ct
        in_specs.append(pl.BlockSpec((FA, BBLK, ct), lambda b, c, off=off: (0, b, off + c)))
    flat_w = []
    for o in range(n_out):
        for k in range(n_in):
            flat_w.append(w_tabs[o][k])
            in_specs.append(pl.BlockSpec((BBLK, m, FA), lambda b, c: (b, 0, 0)))
    return pl.pallas_call(
        functools.partial(_stage_a_body, n_in, n_out),
        grid=(FBP // BBLK, n_cols // ct),
        in_specs=in_specs,
        out_specs=[pl.BlockSpec((BBLK, m, ct), lambda b, c: (b, 0, c)) for _ in range(n_out)],
        out_shape=[jax.ShapeDtypeStruct((FBP, m, n_cols), BF16) for _ in range(n_out)],
        compiler_params=_cparams(("parallel", "parallel")),
        name="dft_stage_a",
    )(*x3, *flat_w)


def _stage_ainv_body(final, tr_ref, ti_ref, wr_ref, wi_ref, v_ref, x_ref, skip_ref, *rest):
    if final:
        gate_ref, o_ref = rest
    else:
        (o_ref,) = rest
    skip = skip_ref[...]
    for j in range(BBLK):
        y = jnp.dot(wr_ref[j], tr_ref[j], preferred_element_type=F32)
        y = y + jnp.dot(wi_ref[j], ti_ref[j], preferred_element_type=F32)
        y = y + v_ref[:, j, :] * skip
        y = x_ref[:, j, :] * y
        if final:
            g = gate_ref[:, j, :]
            y = y * (g * jax.nn.sigmoid(g))
        o_ref[:, j, :] = y


def _stage_ainv(tr, ti, wr, wi, v, v_off, xmul, x_off, skip, gate=None, gate_off=0, ct=1024):
    final = gate is not None
    n_cols = tr.shape[2]
    kap = tr.shape[1]
    view = lambda x: x.reshape(FA, FB, x.shape[1])
    spec3 = lambda off: pl.BlockSpec((FA, BBLK, ct), lambda b, c, off=off // ct: (0, b, off + c))
    in_specs = [
        pl.BlockSpec((BBLK, kap, ct), lambda b, c: (b, 0, c)),
        pl.BlockSpec((BBLK, kap, ct), lambda b, c: (b, 0, c)),
        pl.BlockSpec((BBLK, FA, kap), lambda b, c: (b, 0, 0)),
        pl.BlockSpec((BBLK, FA, kap), lambda b, c: (b, 0, 0)),
        spec3(v_off),
        spec3(x_off),
        pl.BlockSpec((1, ct), lambda b, c: (0, c)),
    ]
    args = [tr, ti, wr, wi, view(v), view(xmul), skip.reshape(1, n_cols)]
    if final:
        in_specs.append(spec3(gate_off))
        args.append(view(gate))
    out = pl.pallas_call(
        functools.partial(_stage_ainv_body, final),
        grid=(FBP // BBLK, n_cols // ct),
        in_specs=in_specs,
        out_specs=pl.BlockSpec((FA, BBLK, ct), lambda b, c: (0, b, c)),
        out_shape=jax.ShapeDtypeStruct((FA, FB, n_cols), F32),
        compiler_params=_cparams(("parallel", "parallel")),
        name="dft_stage_a_inv",
    )(*args)
    return out.reshape(SEQ_T, n_cols)


def _filter_spec_body(fr_ref, fi_ref, br_ref, bi_ref, c_ref, s_ref, kr_ref, ki_ref):
    c = c_ref[...]
    s = s_ref[...]
    dot = lambda a, b: jnp.dot(a, b[...], preferred_element_type=F32)
    kr_ref[...] = dot(c, fr_ref) + dot(s, fi_ref) + dot(c, br_ref) + dot(s, bi_ref)
    ki_ref[...] = dot(c, fi_ref) - dot(s, fr_ref) - dot(c, bi_ref) + dot(s, br_ref)


def _filter_spectrum(tr, ti, cb, sb, ct=1024):
    nc = C_BR // ct
    kap = tr.shape[1]
    n_cols = tr.shape[2]
    tr2 = tr.reshape(FBP, kap * n_cols)
    ti2 = ti.reshape(FBP, kap * n_cols)
    ncol_blk = n_cols // ct

    def tspec(direction):
        return pl.BlockSpec((FBP, ct), lambda o, ka, c: (0, ka * ncol_blk + (2 * o + direction) * nc + c))

    mat = pl.BlockSpec((FBP, FBP), lambda o, ka, c: (0, 0))
    ospec = pl.BlockSpec((None, None, FBP, ct), lambda o, ka, c: (o, ka, 0, c))
    oshape = jax.ShapeDtypeStruct((2, KA_HY, FBP, C_BR), F32)
    return pl.pallas_call(
        _filter_spec_body,
        grid=(2, KA_HY, nc),
        in_specs=[tspec(0), tspec(0), tspec(1), tspec(1), mat, mat],
        out_specs=[ospec, ospec],
        out_shape=[oshape, oshape],
        compiler_params=_cparams(("parallel", "parallel", "parallel")),
        name="filter_spectrum",
    )(tr2, ti2, tr2, ti2, cb, sb)


def _conv_b_body(tr_ref, ti_ref, kr_ref, ki_ref, c_ref, s_ref, or_ref, oi_ref):
    ka = pl.program_id(0)

    @pl.when(ka < KA_HY)
    def _():
        c = c_ref[...]
        s = s_ref[...]
        dot = lambda a, b: jnp.dot(a, b, preferred_element_type=F32)
        tr = tr_ref[...]
        ti = ti_ref[...]
        yr = dot(c, tr) + dot(s, ti)
        yi = dot(c, ti) - dot(s, tr)
        kr = kr_ref[...]
        ki = ki_ref[...]
        zr = (yr * kr - yi * ki).astype(BF16)
        zi = (yr * ki + yi * kr).astype(BF16)
        or_ref[...] = (dot(c, zr) - dot(s, zi)).astype(BF16)
        oi_ref[...] = (dot(c, zi) + dot(s, zr)).astype(BF16)

    @pl.when(ka >= KA_HY)
    def _():
        or_ref[...] = jnp.zeros_like(or_ref)
        oi_ref[...] = jnp.zeros_like(oi_ref)


def _conv_stage_b(tr, ti, kr, ki, order, cb, sb, ct=1024):
    kap = tr.shape[1]
    n_cols = tr.shape[2]
    nc = n_cols // ct
    tr2 = tr.reshape(FBP, kap * n_cols)
    ti2 = ti.reshape(FBP, kap * n_cols)
    tspec = pl.BlockSpec((FBP, ct), lambda ka, c: (0, ka * nc + c))
    kspec = pl.BlockSpec((None, None, FBP, ct), lambda ka, c: (order, jnp.minimum(ka, KA_HY - 1), 0, c))
    mat = pl.BlockSpec((FBP, FBP), lambda ka, c: (0, 0))
    oshape = jax.ShapeDtypeStruct((FBP, kap * n_cols), BF16)
    o_r, o_i = pl.pallas_call(
        _conv_b_body,
        grid=(kap, nc),
        in_specs=[tspec, tspec, kspec, kspec, mat, mat],
        out_specs=[tspec, tspec],
        out_shape=[oshape, oshape],
        compiler_params=_cparams(("parallel", "parallel")),
        name="conv_stage_b",
    )(tr2, ti2, kr, ki, cb, sb)
    return o_r.reshape(FBP, kap, n_cols), o_i.reshape(FBP, kap, n_cols)


def _fnet_b_body(tr_ref, ti_ref, c_ref, s_ref, g_ref, o_ref):
    dot = lambda a, b: jnp.dot(a, b, preferred_element_type=F32)
    y = dot(c_ref[...], tr_ref[...]) + dot(s_ref[...], ti_ref[...])
    g = g_ref[...]
    o_ref[...] = y[:FB, :] * (g * jax.nn.sigmoid(g))


def _fnet_stage_b(tr, ti, cb, sb, z, gate_off, ct=1024):
    n_cols = tr.shape[2]
    nc = n_cols // ct
    tr2 = tr.reshape(FBP, FA * n_cols)
    ti2 = ti.reshape(FBP, FA * n_cols)
    zw = z.shape[1]
    z2 = z.reshape(FB, FA * zw)
    tspec = pl.BlockSpec((FBP, ct), lambda ka, c: (0, ka * nc + c))
    mat = pl.BlockSpec((FBP, FBP), lambda ka, c: (0, 0))
    gspec = pl.BlockSpec((FB, ct), lambda ka, c: (0, (ka * zw + gate_off) // ct + c))
    out = pl.pallas_call(
        _fnet_b_body,
        grid=(FA, nc),
        in_specs=[tspec, tspec, mat, mat, gspec],
        out_specs=pl.BlockSpec((FB, ct), lambda ka, c: (0, ka * nc + c)),
        out_shape=jax.ShapeDtypeStruct((FB, FA * n_cols), F32),
        compiler_params=_cparams(("parallel", "parallel")),
        name="fnet_stage_b",
    )(tr2, ti2, cb, sb, z2)
    return out.reshape(SEQ_T, n_cols)


def _chan_dft_body(x_ref, w_ref, p_ref, q_ref):
    y = jnp.dot(x_ref[...].astype(BF16), w_ref[...], preferred_element_type=F32)
    p_ref[...] = y[:, :256]
    q_ref[...] = y[:, 256:]


def _chan_dft(z, off, chan):
    oshape = jax.ShapeDtypeStruct((SEQ_T, C_BR), F32)
    return pl.pallas_call(
        _chan_dft_body,
        grid=(SEQ_T // ROW_TILE, 4),
        in_specs=[
            pl.BlockSpec((ROW_TILE, 256), lambda i, g: (i, off // 256 + g)),
            pl.BlockSpec((256, 512), lambda i, g: (0, 0)),
        ],
        out_specs=[pl.BlockSpec((ROW_TILE, 256), lambda i, g: (i, g))] * 2,
        out_shape=[oshape, oshape],
        compiler_params=_cparams(("parallel", "parallel")),
        name="fnet_chan_dft",
    )(z, chan)


NA_QROWS = 8
NA_SLAB = 16
NA_SCALE = NA_DH ** -0.5


def _na_main_body(q_ref, k_ref, v_ref, km_ref, vm_ref, bt_ref, mb_ref, g_ref, o_ref):
    rb = pl.program_id(1)
    slab0 = jnp.clip(rb * NA_QROWS - NA_KH // 2, 0, GRID_H - NA_SLAB)
    lane = lax.broadcasted_iota(jnp.int32, (GRID_W, 2 * NA_DH), 1)
    km = km_ref[...].astype(BF16)
    vm = vm_ref[...].astype(BF16)
    dn_t = (((1,), (1,)), ((), ()))
    for i in range(NA_QROWS):
        r = rb * NA_QROWS + i
        r0 = jnp.clip(r - NA_KH // 2, 0, GRID_H - NA_KH)
        off = pl.multiple_of((r0 - slab0) * GRID_W, GRID_W)
        d0 = r0 - r + (NA_KH - 1)
        q = q_ref[i * GRID_W:(i + 1) * GRID_W, :]
        ks = k_ref[pl.ds(off, NA_KH * GRID_W), :].astype(BF16)
        vs = v_ref[pl.ds(off, NA_KH * GRID_W), :].astype(BF16)
        outs = []
        for hh in range(2):
            sel = (lane >= hh * NA_DH) & (lane < (hh + 1) * NA_DH)
            qm = jnp.where(sel, q, 0.0).astype(BF16)
            s = lax.dot_general(qm, ks, dn_t, preferred_element_type=F32) * NA_SCALE + bt_ref[hh, d0]
            sx = lax.dot_general(qm, km, dn_t, preferred_element_type=F32) * NA_SCALE + mb_ref[hh, 0:1, :]
            m = jnp.maximum(jnp.max(s, axis=-1, keepdims=True), jnp.max(sx, axis=-1, keepdims=True))
            p = jnp.exp(s - m)
            px = jnp.exp(sx - m)
            den = jnp.sum(p, axis=-1, keepdims=True) + jnp.sum(px, axis=-1, keepdims=True)
            o = jnp.dot(p.astype(BF16), vs, preferred_element_type=F32)
            o = o + jnp.dot(px.astype(BF16), vm, preferred_element_type=F32)
            outs.append(o / den)
        o = jnp.where(lane < NA_DH, outs[0], outs[1])
        g = g_ref[i * GRID_W:(i + 1) * GRID_W, :]
        o_ref[i * GRID_W:(i + 1) * GRID_W, :] = (o * (g * jax.nn.sigmoid(g))).astype(BF16)


def _na_meta_body(q_ref, k_ref, v_ref, mb_ref, g_ref, o_ref):
    lane = lax.broadcasted_iota(jnp.int32, (N_META, 2 * NA_DH), 1)
    q = q_ref[...]
    km = k_ref[...].astype(BF16)
    vm = v_ref[...].astype(BF16)
    dn_t = (((1,), (1,)), ((), ()))
    outs = []
    for hh in range(2):
        sel = (lane >= hh * NA_DH) & (lane < (hh + 1) * NA_DH)
        qm = jnp.where(sel, q, 0.0).astype(BF16)
        s = lax.dot_general(qm, km, dn_t, preferred_element_type=F32) * NA_SCALE + mb_ref[hh, 0:1, :]
        m = jnp.max(s, axis=-1, keepdims=True)
        p = jnp.exp(s - m)
        den = jnp.sum(p, axis=-1, keepdims=True)
        outs.append(jnp.dot(p.astype(BF16), vm, preferred_element_type=F32) / den)
    o = jnp.where(lane < NA_DH, outs[0], outs[1])
    g = g_ref[...]
    o_ref[...] = (o * (g * jax.nn.sigmoid(g))).astype(BF16)


def _na_bias_table(rpb):
    colmask, dc = _na_index_tables()
    rows = np.arange(NA_KH)[:, None] + np.arange(NA_KH)[None, :]
    t = rpb.astype(F32)[:, rows]
    t = t[:, :, :, dc]
    t = jnp.where(colmask[None, None, None], t, NEG_INF)
    t = jnp.transpose(t, (0, 1, 3, 2, 4)).reshape(NA_HEADS, NA_KH, GRID_W, NA_KH * GRID_W)
    return t.reshape(NA_HEADS // 2, 2, NA_KH, GRID_W, NA_KH * GRID_W)


def _neighborhood_attention(z, rpb, meta_bias):
    bt = _na_bias_table(rpb)
    mb = jnp.broadcast_to(meta_bias.astype(F32).reshape(NA_HEADS // 2, 2, 1, N_META),
                          (NA_HEADS // 2, 2, 8, N_META))
    w2 = 2 * NA_DH
    qrows = NA_QROWS * GRID_W
    srows = NA_SLAB * GRID_W

    def slab_start(rb):
        start = N_META + GRID_W * jnp.clip(rb * NA_QROWS - NA_KH // 2, 0, GRID_H - NA_SLAB)
        return pl.multiple_of(start, N_META)

    def col(off):
        return lambda hp, rb: (pl.multiple_of(N_META + rb * qrows, N_META), pl.multiple_of(off + hp * w2, w2))

    def slab(off):
        return lambda hp, rb: (slab_start(rb), pl.multiple_of(off + hp * w2, w2))

    def meta(off):
        return lambda hp, rb: (0, off // w2 + hp)

    y_main = pl.pallas_call(
        _na_main_body,
        grid=(NA_HEADS // 2, GRID_H // NA_QROWS),
        in_specs=[
            pl.BlockSpec((pl.Element(qrows), pl.Element(w2)), col(OFF_Q)),
            pl.BlockSpec((pl.Element(srows), pl.Element(w2)), slab(OFF_K)),
            pl.BlockSpec((pl.Element(srows), pl.Element(w2)), slab(OFF_V)),
            pl.BlockSpec((N_META, w2), meta(OFF_K)),
            pl.BlockSpec((N_META, w2), meta(OFF_V)),
            pl.BlockSpec((None, 2, NA_KH, GRID_W, NA_KH * GRID_W), lambda hp, rb: (hp, 0, 0, 0, 0)),
            pl.BlockSpec((None, 2, 8, N_META), lambda hp, rb: (hp, 0, 0, 0)),
            pl.BlockSpec((pl.Element(qrows), pl.Element(w2)), col(OFF_NA_GATE)),
        ],
        out_specs=pl.BlockSpec((qrows, w2), lambda hp, rb: (rb, hp)),
        out_shape=jax.ShapeDtypeStruct((SEQ, C_BR), BF16),
        compiler_params=_cparams(("parallel", "parallel")),
        name="na_main",
    )(z, z, z, z, z, bt, mb, z)

    def mcol(off):
        return lambda hp: (0, off // w2 + hp)

    y_meta = pl.pallas_call(
        _na_meta_body,
        grid=(NA_HEADS // 2,),
        in_specs=[
            pl.BlockSpec((N_META, w2), mcol(OFF_Q)),
            pl.BlockSpec((N_META, w2), mcol(OFF_K)),
            pl.BlockSpec((N_META, w2), mcol(OFF_V)),
            pl.BlockSpec((None, 2, 8, N_META), lambda hp: (hp, 0, 0, 0)),
            pl.BlockSpec((N_META, w2), mcol(OFF_NA_GATE)),
        ],
        out_specs=pl.BlockSpec((N_META, w2), lambda hp: (0, hp)),
        out_shape=jax.ShapeDtypeStruct((N_META, C_BR), BF16),
        compiler_params=_cparams(("parallel",)),
        name="na_meta",
    )(z, z, z, mb, z)
    return y_meta, y_main


def _merge_body(ya_ref, yb_ref, yc_ref, wa_ref, wb_ref, wc_ref, ga_ref, gb_ref, gc_ref, o_ref):
    dot = lambda a, b: jnp.dot(a, b[...], preferred_element_type=F32)
    m = jax.nn.sigmoid(ga_ref[...]) * dot(ya_ref[...].astype(BF16), wa_ref)
    m = m + jax.nn.sigmoid(gb_ref[...]) * dot(yb_ref[...].astype(BF16), wb_ref)
    m = m + jax.nn.sigmoid(gc_ref[...]) * dot(yc_ref[...], wc_ref)
    o_ref[...] = m.astype(BF16)


def _merge(ya, yb, yc, wa, wb, wc, z, tn=1024):
    n_rows = ya.shape[0]
    yspec = pl.BlockSpec((ROW_TILE, C_BR), lambda i, j: (i, 0))
    wspec = pl.BlockSpec((C_BR, tn), lambda i, j: (0, j))
    gspec = lambda k: pl.BlockSpec((ROW_TILE, tn), lambda i, j, k=k: (i, (OFF_MERGE + k * D_MODEL) // tn + j))
    return pl.pallas_call(
        _merge_body,
        grid=(n_rows // ROW_TILE, D_MODEL // tn),
        in_specs=[yspec, yspec, yspec, wspec, wspec, wspec, gspec(0), gspec(1), gspec(2)],
        out_specs=pl.BlockSpec((ROW_TILE, tn), lambda i, j: (i, j)),
        out_shape=jax.ShapeDtypeStruct((n_rows, D_MODEL), BF16),
        compiler_params=_cparams(("parallel", "parallel")),
        name="merge",
    )(ya, yb, yc, wa, wb, wc, z, z, z)


def _outproj_body(h_ref, m_ref, w_ref, o_ref):
    o_ref[...] = h_ref[...] + jnp.dot(m_ref[...], w_ref[...], preferred_element_type=F32)


def _outproj(h, m, w, tn=1024):
    n_rows = h.shape[0]
    return pl.pallas_call(
        _outproj_body,
        grid=(n_rows // ROW_TILE, D_MODEL // tn),
        in_specs=[
            pl.BlockSpec((ROW_TILE, tn), lambda i, j: (i, j)),
            pl.BlockSpec((ROW_TILE, D_MODEL), lambda i, j: (i, 0)),
            pl.BlockSpec((D_MODEL, tn), lambda i, j: (0, j)),
        ],
        out_specs=pl.BlockSpec((ROW_TILE, tn), lambda i, j: (i, j)),
        out_shape=jax.ShapeDtypeStruct((n_rows, D_MODEL), F32),
        compiler_params=_cparams(("parallel", "parallel")),
        name="outproj",
    )(h, m, w)


def _final_norm_body(x_ref, g_ref, o_ref):
    x = x_ref[...]
    y = x * lax.rsqrt(jnp.mean(x * x, axis=-1, keepdims=True) + EPS)
    o_ref[...] = y * g_ref[...]


def _final_norm(h, g, tm=512):
    return pl.pallas_call(
        _final_norm_body,
        grid=(SEQ // tm,),
        in_specs=[
            pl.BlockSpec((pl.Element(tm), pl.Element(D_MODEL)), lambda i: (pl.multiple_of(N_META + i * tm, N_META), 0)),
            pl.BlockSpec((1, D_MODEL), lambda i: (0, 0)),
        ],
        out_specs=pl.BlockSpec((tm, D_MODEL), lambda i: (i, 0)),
        out_shape=jax.ShapeDtypeStruct((SEQ, D_MODEL), F32),
        compiler_params=_cparams(("parallel",)),
        name="final_norm",
    )(h, g.reshape(1, D_MODEL))


def _hyena_branch(z, conv_w, conv_b, w1, b1, w2, b2, w3, b3, w4, freq, decay, skip):
    tab = _hyena_tables()
    fwd = [[_bf(tab["fwd_r"])], [_bf(tab["fwd_i"])]]
    inv_r, inv_i = _bf(tab["inv_r"]), _bf(tab["inv_i"])
    cb, sb = _bf(tab["cb"]), _bf(tab["sb"])

    hyc = _shortconv(z, conv_w, conv_b)
    filt = _hyena_filters(w1, b1, w2, b2, w3, b3, w4, freq, decay)
    ftr, fti = _stage_a([filt], [0], filt.shape[1], fwd)
    kr, ki = _filter_spectrum(ftr, fti, cb, sb)

    tr, ti = _stage_a([hyc], [0], C_BR, fwd)
    tr, ti = _conv_stage_b(tr, ti, kr, ki, 0, cb, sb)
    zmid = _stage_ainv(tr, ti, inv_r, inv_i, hyc, 0, hyc, C_BR, skip[0])

    tr, ti = _stage_a([zmid], [0], C_BR, fwd)
    tr, ti = _conv_stage_b(tr, ti, kr, ki, 1, cb, sb)
    return _stage_ainv(tr, ti, inv_r, inv_i, zmid, 0, hyc, 2 * C_BR, skip[1], gate=z, gate_off=OFF_HY_GATE)


def _fnet_branch(z):
    tab = _fnet_tables()
    cs, sn = _bf(tab["cs"]), _bf(tab["sn"])
    p, q = _chan_dft(z, OFF_FN_IN, _bf(tab["chan"]))
    tr, ti = _stage_a([p, q], [0, 0], C_BR, [[cs, -sn], [-sn, -cs]])
    return _fnet_stage_b(tr, ti, _bf(tab["cb"]), _bf(tab["sb"]), z, OFF_FN_GATE)


def _layer(h, norm_g, w_in, conv_w, conv_b, w1, b1, w2, b2, w3, b3, w4, freq, decay, skip, rpb, meta_bias,
           w_a, w_b, w_c, w_out):
    z = _inproj(h, norm_g, w_in.astype(BF16))
    ya = _hyena_branch(z, conv_w, conv_b, w1, b1, w2, b2, w3, b3, w4, freq, decay, skip)
    yb = _fnet_branch(z)
    yc_meta, yc_main = _neighborhood_attention(z, rpb, meta_bias)
    yc = jnp.concatenate([yc_meta, yc_main], axis=0)
    m = _merge(ya, yb, yc, w_a.astype(BF16), w_b.astype(BF16), w_c.astype(BF16), z)
    return _outproj(h, m, w_out.astype(BF16))


def kernel(x, meta_tokens, norm_g, w_in, hy_conv_w, hy_conv_b, hy_flt_w1, hy_flt_b1, hy_flt_w2, hy_flt_b2,
           hy_flt_w3, hy_flt_b3, hy_flt_w4, hy_flt_freq, hy_decay, hy_skip, na_rpb, na_meta_bias,
           w_branch_a, w_branch_b, w_branch_c, w_out, final_g):
    assert x.shape == (1, SEQ, D_MODEL)
    h = jnp.concatenate([meta_tokens.astype(x.dtype), x[0]], axis=0)
    for i in range(norm_g.shape[0]):
        h = _layer(h, norm_g[i], w_in[i], hy_conv_w[i], hy_conv_b[i], hy_flt_w1[i], hy_flt_b1[i],
                   hy_flt_w2[i], hy_flt_b2[i], hy_flt_w3[i], hy_flt_b3[i], hy_flt_w4[i], hy_flt_freq[i],
                   hy_decay[i].reshape(-1), hy_skip[i], na_rpb[i], na_meta_bias[i],
                   w_branch_a[i], w_branch_b[i], w_branch_c[i], w_out[i])
    return _final_norm(h, final_g)[None]
```

```python
import functools
import math

import numpy as np
import jax
import jax.numpy as jnp
from jax import lax
from jax.experimental import pallas as pl
from jax.experimental.pallas import tpu as pltpu

F32 = jnp.float32
BF16 = jnp.bfloat16

D_MODEL = 2048
SEQ = 16384
N_META = 16
SEQ_T = SEQ + N_META
GRID_W = 64
GRID_H = SEQ // GRID_W
C_BR = 1024
HY_EMB = 33
HY_HID = 64
NA_HEADS = 16
NA_DH = 64
NA_KH = 8
NA_KW = 16
EPS = 1e-6
NEG_INF = -1e30

OFF_HY_IN = 0
OFF_HY_GATE = 3072
OFF_FN_IN = 4096
OFF_FN_GATE = 5120
OFF_Q = 6144
OFF_K = 7168
OFF_V = 8192
OFF_NA_GATE = 9216
OFF_MERGE = 10240
N_IN = 16384

FA = 80
FB = 205
FBP = 208
BBLK = 8
KA_HY = FA + 1
KAP_HY = 96

ROW_TILE = 656
LANE = 128
VMEM_LIMIT = 48 * 1024 * 1024


def _cparams(sem):
    return pltpu.CompilerParams(dimension_semantics=sem, vmem_limit_bytes=VMEM_LIMIT)


@functools.lru_cache(maxsize=None)
def _hyena_tables():
    n_circ = 2 * SEQ_T
    a = np.arange(FA)[None, None, :]
    b = np.arange(FBP)[:, None, None]
    ka = np.arange(KAP_HY)[None, :, None]
    n = FB * a + b
    ang = 2.0 * np.pi * ((ka * n) % n_circ) / n_circ
    valid = (b < FB) & (ka < KA_HY)
    fwd_r = np.where(valid, np.cos(ang), 0.0)
    fwd_i = np.where(valid, -np.sin(ang), 0.0)
    c = np.where((ka == 0) | (ka == FA), 1.0, 2.0) / n_circ
    inv_r = np.transpose(np.where(valid, c * np.cos(ang), 0.0), (0, 2, 1))
    inv_i = np.transpose(np.where(valid, -c * np.sin(ang), 0.0), (0, 2, 1))
    kb = np.arange(FBP)[:, None]
    bb = np.arange(FBP)[None, :]
    phi = 2.0 * np.pi * ((kb * bb) % FB) / FB
    ok = (kb < FB) & (bb < FB)
    cb = np.where(ok, np.cos(phi), 0.0)
    sb = np.where(ok, np.sin(phi), 0.0)
    return dict(fwd_r=fwd_r, fwd_i=fwd_i, inv_r=inv_r, inv_i=inv_i, cb=cb, sb=sb)


@functools.lru_cache(maxsize=None)
def _fnet_tables():
    a = np.arange(FA)[None, None, :]
    b = np.arange(FBP)[:, None, None]
    ka = np.arange(FA)[None, :, None]
    n = FB * a + b
    ang = 2.0 * np.pi * ((ka * n) % SEQ_T) / SEQ_T
    valid = np.broadcast_to(b < FB, ang.shape)
    cs = np.where(valid, np.cos(ang), 0.0)
    sn = np.where(valid, np.sin(ang), 0.0)
    kb = np.arange(FBP)[:, None]
    bb = np.arange(FBP)[None, :]
    phi = 2.0 * np.pi * ((kb * bb) % FB) / FB
    ok = (kb < FB) & (bb < FB)
    scale = 1.0 / math.sqrt(SEQ_T * 256.0)
    cb = np.where(ok, np.cos(phi), 0.0) * scale
    sb = np.where(ok, np.sin(phi), 0.0) * scale
    j = np.arange(256)
    th = 2.0 * np.pi * ((j[:, None] * j[None, :]) % 256) / 256.0
    chan = np.concatenate([np.cos(th), np.sin(th)], axis=1)
    return dict(cs=cs, sn=sn, cb=cb, sb=sb, chan=chan)


@functools.lru_cache(maxsize=None)
def _filter_features():
    t = np.linspace(0.0, 1.0, SEQ_T)[:, None]
    bands = (HY_EMB - 1) // 2
    w = 2.0 * np.pi * np.arange(SEQ_T)[:, None] / SEQ_T
    f = np.linspace(1e-4, bands - 1, bands)[None, :]
    z = np.concatenate([t, np.cos(f * w), -np.sin(f * w)], axis=-1)
    out = np.zeros((SEQ_T, HY_HID), np.float64)
    out[:, :HY_EMB] = z
    return out


@functools.lru_cache(maxsize=None)
def _na_index_tables():
    qc = np.arange(GRID_W)[:, None]
    kc = np.arange(GRID_W)[None, :]
    cs = np.clip(qc - NA_KW // 2, 0, GRID_W - NA_KW)
    colmask = (kc >= cs) & (kc < cs + NA_KW)
    dc = np.clip(kc - qc + NA_KW - 1, 0, 2 * NA_KW - 2)
    return colmask, dc


def _bf(x):
    return jnp.asarray(np.asarray(x, np.float32), dtype=BF16)


def _inproj_body(x_ref, g_ref, w_ref, o_ref, xn_ref):
    @pl.when(pl.program_id(1) == 0)
    def _():
        x = x_ref[...]
        y = x * lax.rsqrt(jnp.mean(x * x, axis=-1, keepdims=True) + EPS)
        xn_ref[...] = (y * g_ref[...]).astype(BF16)

    o_ref[...] = jnp.dot(xn_ref[...], w_ref[...], preferred_element_type=F32)


def _inproj(h, g, w_bf16, tn=1024):
    n_rows, d = h.shape
    n_out = w_bf16.shape[1]
    return pl.pallas_call(
        _inproj_body,
        grid=(n_rows // ROW_TILE, n_out // tn),
        in_specs=[
            pl.BlockSpec((ROW_TILE, d), lambda i, j: (i, 0)),
            pl.BlockSpec((1, d), lambda i, j: (0, 0)),
            pl.BlockSpec((d, tn), lambda i, j: (0, j)),
        ],
        out_specs=pl.BlockSpec((ROW_TILE, tn), lambda i, j: (i, j)),
        out_shape=jax.ShapeDtypeStruct((n_rows, n_out), F32),
        scratch_shapes=[pltpu.VMEM((ROW_TILE, d), BF16)],
        compiler_params=_cparams(("parallel", "arbitrary")),
        name="inproj",
    )(h, g.reshape(1, d), w_bf16)


def _shortconv_body(prev_ref, cur_ref, next_ref, w_ref, b_ref, o_ref):
    i = pl.program_id(0)
    last = pl.num_programs(0) - 1
    x = cur_ref[...]
    rows = lax.broadcasted_iota(jnp.int32, x.shape, 0)
    prev_row = jnp.where(i == 0, 0.0, prev_ref[7:8, :])
    next_row = jnp.where(i == last, 0.0, next_ref[0:1, :])
    up = jnp.where(rows == 0, prev_row, pltpu.roll(x, 1, 0))
    dn = jnp.where(rows == ROW_TILE - 1, next_row, pltpu.roll(x, ROW_TILE - 1, 0))
    y = up * w_ref[0:1, :]
    y = y + x * w_ref[1:2, :]
    y = y + dn * w_ref[2:3, :]
    o_ref[...] = y + b_ref[...]


def _shortconv(z, w, b, tc=1024):
    n_rows = z.shape[0]
    n_c = w.shape[1]
    n8 = n_rows // 8
    r8 = ROW_TILE // 8
    return pl.pallas_call(
        _shortconv_body,
        grid=(n_rows // ROW_TILE, n_c // tc),
        in_specs=[
            pl.BlockSpec((8, tc), lambda i, j: (jnp.maximum(i * r8 - 1, 0), j)),
            pl.BlockSpec((ROW_TILE, tc), lambda i, j: (i, j)),
            pl.BlockSpec((8, tc), lambda i, j: (jnp.minimum((i + 1) * r8, n8 - 1), j)),
            pl.BlockSpec((3, tc), lambda i, j: (0, j)),
            pl.BlockSpec((1, tc), lambda i, j: (0, j)),
        ],
        out_specs=pl.BlockSpec((ROW_TILE, tc), lambda i, j: (i, j)),
        out_shape=jax.ShapeDtypeStruct((n_rows, n_c), F32),
        compiler_params=_cparams(("parallel", "parallel")),
        name="shortconv",
    )(z, z, z, w, b.reshape(1, n_c))


def _hp_dot(a, b):
    return jnp.dot(a, b, preferred_element_type=F32, precision=lax.Precision.HIGHEST)


def _filter_body(z_ref, w1_ref, b1_ref, w2_ref, b2_ref, w3_ref, b3_ref, fr_ref, w4_ref, dec_ref, o_ref, h_ref):
    i = pl.program_id(0)
    j = pl.program_id(1)

    @pl.when(j == 0)
    def _():
        fr = fr_ref[...]
        h = jnp.sin(fr * (_hp_dot(z_ref[...], w1_ref[...]) + b1_ref[...]))
        h = jnp.sin(fr * (_hp_dot(h, w2_ref[...]) + b2_ref[...]))
        h_ref[...] = jnp.sin(fr * (_hp_dot(h, w3_ref[...]) + b3_ref[...]))

    t = z_ref[:, 0:1]
    y = _hp_dot(h_ref[...], w4_ref[...]) * jnp.exp(-t * jnp.abs(dec_ref[...]))
    rows = lax.broadcasted_iota(jnp.int32, y.shape, 0)
    drop = jnp.logical_and(jnp.logical_and(i == 0, j % 2 == 1), rows == 0)
    o_ref[...] = jnp.where(drop, 0.0, y)


def _hyena_filters(w1, b1, w2, b2, w3, b3, w4, freq, decay):
    zfeat = jnp.asarray(_filter_features(), dtype=F32)
    w1p = jnp.zeros((HY_HID, HY_HID), F32).at[:HY_EMB].set(w1)
    n_c = w4.shape[1]
    tc = C_BR
    small = lambda i, j: (0, 0)
    return pl.pallas_call(
        _filter_body,
        grid=(SEQ_T // ROW_TILE, n_c // tc),
        in_specs=[
            pl.BlockSpec((ROW_TILE, HY_HID), lambda i, j: (i, 0)),
            pl.BlockSpec((HY_HID, HY_HID), small),
            pl.BlockSpec((1, HY_HID), small),
            pl.BlockSpec((HY_HID, HY_HID), small),
            pl.BlockSpec((1, HY_HID), small),
            pl.BlockSpec((HY_HID, HY_HID), small),
            pl.BlockSpec((1, HY_HID), small),
            pl.BlockSpec((1, HY_HID), small),
            pl.BlockSpec((HY_HID, tc), lambda i, j: (0, j)),
            pl.BlockSpec((1, tc), lambda i, j: (0, j)),
        ],
        out_specs=pl.BlockSpec((ROW_TILE, tc), lambda i, j: (i, j)),
        out_shape=jax.ShapeDtypeStruct((SEQ_T, n_c), F32),
        scratch_shapes=[pltpu.VMEM((ROW_TILE, HY_HID), F32)],
        compiler_params=_cparams(("parallel", "arbitrary")),
        name="hyena_filter",
    )(zfeat, w1p, b1.reshape(1, -1), w2, b2.reshape(1, -1), w3, b3.reshape(1, -1), freq.reshape(1, -1),
      w4, decay.reshape(1, n_c))


def _stage_a_body(n_in, n_out, *refs):
    x_refs = refs[:n_in]
    w_refs = refs[n_in:n_in + n_in * n_out]
    o_refs = refs[n_in + n_in * n_out:]
    bb = pl.program_id(0)
    for j in range(BBLK):
        valid = bb * BBLK + j < FB
        xs = [jnp.where(valid, xr[:, j, :], 0.0).astype(BF16) for xr in x_refs]
        for o in range(n_out):
            acc = None
            for k in range(n_in):
                d = jnp.dot(w_refs[o * n_in + k][j], xs[k], preferred_element_type=F32)
                acc = d if acc is None else acc + d
            o_refs[o][j] = acc.astype(BF16)


def _stage_a(xs, col_offs, n_cols, w_tabs, ct=1024):
    n_in = len(xs)
    n_out = len(w_tabs)
    m = w_tabs[0][0].shape[1]
    x3 = [x.reshape(FA, FB, x.shape[1]) for x in xs]
    in_specs = []
    for k in range(n_in):
        off = col_offs[k] // ct
        in_specs.append(pl.BlockSpec((FA, BBLK, ct), lambda b, c, off=off: (0, b, off + c)))
    flat_w = []
    for o in range(n_out):
        for k in range(n_in):
            flat_w.append(w_tabs[o][k])
            in_specs.append(pl.BlockSpec((BBLK, m, FA), lambda b, c: (b, 0, 0)))
    return pl.pallas_call(
        functools.partial(_stage_a_body, n_in, n_out),
        grid=(FBP // BBLK, n_cols // ct),
        in_specs=in_specs,
        out_specs=[pl.BlockSpec((BBLK, m, ct), lambda b, c: (b, 0, c)) for _ in range(n_out)],
        out_shape=[jax.ShapeDtypeStruct((FBP, m, n_cols), BF16) for _ in range(n_out)],
        compiler_params=_cparams(("parallel", "parallel")),
        name="dft_stage_a",
    )(*x3, *flat_w)


def _stage_ainv_body(tr_ref, ti_ref, wr_ref, wi_ref, v_ref, x_ref, skip_ref, o_ref):
    skip = skip_ref[...]
    for j in range(BBLK):
        y = jnp.dot(wr_ref[j], tr_ref[j], preferred_element_type=F32)
        y = y + jnp.dot(wi_ref[j], ti_ref[j], preferred_element_type=F32)
        y = y + v_ref[:, j, :] * skip
        o_ref[:, j, :] = x_ref[:, j, :] * y


def _stage_ainv(tr, ti, wr, wi, v, v_off, xmul, x_off, skip, ct=1024):
    n_cols = tr.shape[2]
    kap = tr.shape[1]
    view = lambda x: x.reshape(FA, FB, x.shape[1])
    spec3 = lambda off: pl.BlockSpec((FA, BBLK, ct), lambda b, c, off=off // ct: (0, b, off + c))
    in_specs = [
        pl.BlockSpec((BBLK, kap, ct), lambda b, c: (b, 0, c)),
        pl.BlockSpec((BBLK, kap, ct), lambda b, c: (b, 0, c)),
        pl.BlockSpec((BBLK, FA, kap), lambda b, c: (b, 0, 0)),
        pl.BlockSpec((BBLK, FA, kap), lambda b, c: (b, 0, 0)),
        spec3(v_off),
        spec3(x_off),
        pl.BlockSpec((1, ct), lambda b, c: (0, c)),
    ]
    args = [tr, ti, wr, wi, view(v), view(xmul), skip.reshape(1, n_cols)]
    out = pl.pallas_call(
        _stage_ainv_body,
        grid=(FBP // BBLK, n_cols // ct),
        in_specs=in_specs,
        out_specs=pl.BlockSpec((FA, BBLK, ct), lambda b, c: (0, b, c)),
        out_shape=jax.ShapeDtypeStruct((FA, FB, n_cols), F32),
        compiler_params=_cparams(("parallel", "parallel")),
        name="dft_stage_a_inv",
    )(*args)
    return out.reshape(SEQ_T, n_cols)


def _filter_spec_body(fr_ref, fi_ref, br_ref, bi_ref, c_ref, s_ref, kr_ref, ki_ref):
    c = c_ref[...]
    s = s_ref[...]
    dot = lambda a, b: jnp.dot(a, b[...], preferred_element_type=F32)
    kr_ref[...] = dot(c, fr_ref) + dot(s, fi_ref) + dot(c, br_ref) + dot(s, bi_ref)
    ki_ref[...] = dot(c, fi_ref) - dot(s, fr_ref) - dot(c, bi_ref) + dot(s, br_ref)


def _filter_spectrum(tr, ti, cb, sb, ct=1024):
    nc = C_BR // ct
    kap = tr.shape[1]
    n_cols = tr.shape[2]
    tr2 = tr.reshape(FBP, kap * n_cols)
    ti2 = ti.reshape(FBP, kap * n_cols)
    ncol_blk = n_cols // ct

    def tspec(direction):
        return pl.BlockSpec((FBP, ct), lambda o, ka, c: (0, ka * ncol_blk + (2 * o + direction) * nc + c))

    mat = pl.BlockSpec((FBP, FBP), lambda o, ka, c: (0, 0))
    ospec = pl.BlockSpec((None, None, FBP, ct), lambda o, ka, c: (o, ka, 0, c))
    oshape = jax.ShapeDtypeStruct((2, KA_HY, FBP, C_BR), F32)
    return pl.pallas_call(
        _filter_spec_body,
        grid=(2, KA_HY, nc),
        in_specs=[tspec(0), tspec(0), tspec(1), tspec(1), mat, mat],
        out_specs=[ospec, ospec],
        out_shape=[oshape, oshape],
        compiler_params=_cparams(("parallel", "parallel", "parallel")),
        name="filter_spectrum",
    )(tr2, ti2, tr2, ti2, cb, sb)


def _conv_b_body(tr_ref, ti_ref, kr_ref, ki_ref, c_ref, s_ref, or_ref, oi_ref):
    ka = pl.program_id(0)

    @pl.when(ka < KA_HY)
    def _():
        c = c_ref[...]
        s = s_ref[...]
        dot = lambda a, b: jnp.dot(a, b, preferred_element_type=F32)
        tr = tr_ref[...]
        ti = ti_ref[...]
        yr = dot(c, tr) + dot(s, ti)
        yi = dot(c, ti) - dot(s, tr)
        kr = kr_ref[...]
        ki = ki_ref[...]
        zr = (yr * kr - yi * ki).astype(BF16)
        zi = (yr * ki + yi * kr).astype(BF16)
        or_ref[...] = (dot(c, zr) - dot(s, zi)).astype(BF16)
        oi_ref[...] = (dot(c, zi) + dot(s, zr)).astype(BF16)

    @pl.when(ka >= KA_HY)
    def _():
        or_ref[...] = jnp.zeros_like(or_ref)
        oi_ref[...] = jnp.zeros_like(oi_ref)


def _conv_stage_b(tr, ti, kr, ki, order, cb, sb, ct=1024):
    kap = tr.shape[1]
    n_cols = tr.shape[2]
    nc = n_cols // ct
    tr2 = tr.reshape(FBP, kap * n_cols)
    ti2 = ti.reshape(FBP, kap * n_cols)
    tspec = pl.BlockSpec((FBP, ct), lambda ka, c: (0, ka * nc + c))
    kspec = pl.BlockSpec((None, None, FBP, ct), lambda ka, c: (order, jnp.minimum(ka, KA_HY - 1), 0, c))
    mat = pl.BlockSpec((FBP, FBP), lambda ka, c: (0, 0))
    oshape = jax.ShapeDtypeStruct((FBP, kap * n_cols), BF16)
    o_r, o_i = pl.pallas_call(
        _conv_b_body,
        grid=(kap, nc),
        in_specs=[tspec, tspec, kspec, kspec, mat, mat],
        out_specs=[tspec, tspec],
        out_shape=[oshape, oshape],
        compiler_params=_cparams(("parallel", "parallel")),
        name="conv_stage_b",
    )(tr2, ti2, kr, ki, cb, sb)
    return o_r.reshape(FBP, kap, n_cols), o_i.reshape(FBP, kap, n_cols)


def _fnet_b_body(tr_ref, ti_ref, c_ref, s_ref, o_ref):
    dot = lambda a, b: jnp.dot(a, b, preferred_element_type=F32)
    y = dot(c_ref[...], tr_ref[...]) + dot(s_ref[...], ti_ref[...])
    o_ref[...] = y[:FB, :]


def _fnet_stage_b(tr, ti, cb, sb, ct=1024):
    n_cols = tr.shape[2]
    nc = n_cols // ct
    tr2 = tr.reshape(FBP, FA * n_cols)
    ti2 = ti.reshape(FBP, FA * n_cols)
    tspec = pl.BlockSpec((FBP, ct), lambda ka, c: (0, ka * nc + c))
    mat = pl.BlockSpec((FBP, FBP), lambda ka, c: (0, 0))
    out = pl.pallas_call(
        _fnet_b_body,
        grid=(FA, nc),
        in_specs=[tspec, tspec, mat, mat],
        out_specs=pl.BlockSpec((FB, ct), lambda ka, c: (0, ka * nc + c)),
        out_shape=jax.ShapeDtypeStruct((FB, FA * n_cols), F32),
        compiler_params=_cparams(("parallel", "parallel")),
        name="fnet_stage_b",
    )(tr2, ti2, cb, sb)
    return out.reshape(SEQ_T, n_cols)


def _chan_dft_body(x_ref, w_ref, p_ref, q_ref):
    y = jnp.dot(x_ref[...].astype(BF16), w_ref[...], preferred_element_type=F32)
    p_ref[...] = y[:, :256]
    q_ref[...] = y[:, 256:]


def _chan_dft(z, off, chan):
    oshape = jax.ShapeDtypeStruct((SEQ_T, C_BR), F32)
    return pl.pallas_call(
        _chan_dft_body,
        grid=(SEQ_T // ROW_TILE, 4),
        in_specs=[
            pl.BlockSpec((ROW_TILE, 256), lambda i, g: (i, off // 256 + g)),
            pl.BlockSpec((256, 512), lambda i, g: (0, 0)),
        ],
        out_specs=[pl.BlockSpec((ROW_TILE, 256), lambda i, g: (i, g))] * 2,
        out_shape=[oshape, oshape],
        compiler_params=_cparams(("parallel", "parallel")),
        name="fnet_chan_dft",
    )(z, chan)


NA_QROWS = 8
NA_SLAB = 16
NA_SCALE = NA_DH ** -0.5


def _na_main_body(q_ref, k_ref, v_ref, km_ref, vm_ref, bt_ref, mb_ref, g_ref, o_ref):
    rb = pl.program_id(1)
    slab0 = jnp.clip(rb * NA_QROWS - NA_KH // 2, 0, GRID_H - NA_SLAB)
    lane = lax.broadcasted_iota(jnp.int32, (GRID_W, 2 * NA_DH), 1)
    km = km_ref[...].astype(BF16)
    vm = vm_ref[...].astype(BF16)
    dn_t = (((1,), (1,)), ((), ()))
    for i in range(NA_QROWS):
        r = rb * NA_QROWS + i
        r0 = jnp.clip(r - NA_KH // 2, 0, GRID_H - NA_KH)
        off = pl.multiple_of((r0 - slab0) * GRID_W, GRID_W)
        d0 = r0 - r + (NA_KH - 1)
        q = q_ref[i * GRID_W:(i + 1) * GRID_W, :]
        ks = k_ref[pl.ds(off, NA_KH * GRID_W), :].astype(BF16)
        vs = v_ref[pl.ds(off, NA_KH * GRID_W), :].astype(BF16)
        outs = []
        for hh in range(2):
            sel = (lane >= hh * NA_DH) & (lane < (hh + 1) * NA_DH)
            qm = jnp.where(sel, q, 0.0).astype(BF16)
            s = lax.dot_general(qm, ks, dn_t, preferred_element_type=F32) * NA_SCALE + bt_ref[hh, d0]
            sx = lax.dot_general(qm, km, dn_t, preferred_element_type=F32) * NA_SCALE + mb_ref[hh, 0:1, :]
            m = jnp.maximum(jnp.max(s, axis=-1, keepdims=True), jnp.max(sx, axis=-1, keepdims=True))
            p = jnp.exp(s - m)
            px = jnp.exp(sx - m)
            den = jnp.sum(p, axis=-1, keepdims=True) + jnp.sum(px, axis=-1, keepdims=True)
            o = jnp.dot(p.astype(BF16), vs, preferred_element_type=F32)
            o = o + jnp.dot(px.astype(BF16), vm, preferred_element_type=F32)
            outs.append(o / den)
        o = jnp.where(lane < NA_DH, outs[0], outs[1])
        g = g_ref[i * GRID_W:(i + 1) * GRID_W, :]
        o_ref[i * GRID_W:(i + 1) * GRID_W, :] = (o * (g * jax.nn.sigmoid(g))).astype(BF16)


def _na_meta_body(q_ref, k_ref, v_ref, mb_ref, g_ref, o_ref):
    lane = lax.broadcasted_iota(jnp.int32, (N_META, 2 * NA_DH), 1)
    q = q_ref[...]
    km = k_ref[...].astype(BF16)
    vm = v_ref[...].astype(BF16)
    dn_t = (((1,), (1,)), ((), ()))
    outs = []
    for hh in range(2):
        sel = (lane >= hh * NA_DH) & (lane < (hh + 1) * NA_DH)
        qm = jnp.where(sel, q, 0.0).astype(BF16)
        s = lax.dot_general(qm, km, dn_t, preferred_element_type=F32) * NA_SCALE + mb_ref[hh, 0:1, :]
        m = jnp.max(s, axis=-1, keepdims=True)
        p = jnp.exp(s - m)
        den = jnp.sum(p, axis=-1, keepdims=True)
        outs.append(jnp.dot(p.astype(BF16), vm, preferred_element_type=F32) / den)
    o = jnp.where(lane < NA_DH, outs[0], outs[1])
    g = g_ref[...]
    o_ref[...] = (o * (g * jax.nn.sigmoid(g))).astype(BF16)


def _na_bias_table(rpb):
    colmask, dc = _na_index_tables()
    rows = np.arange(NA_KH)[:, None] + np.arange(NA_KH)[None, :]
    t = rpb.astype(F32)[:, rows]
    t = t[:, :, :, dc]
    t = jnp.where(colmask[None, None, None], t, NEG_INF)
    t = jnp.transpose(t, (0, 1, 3, 2, 4)).reshape(NA_HEADS, NA_KH, GRID_W, NA_KH * GRID_W)
    return t.reshape(NA_HEADS // 2, 2, NA_KH, GRID_W, NA_KH * GRID_W)


def _neighborhood_attention(z, rpb, meta_bias):
    bt = _na_bias_table(rpb)
    mb = jnp.broadcast_to(meta_bias.astype(F32).reshape(NA_HEADS // 2, 2, 1, N_META),
                          (NA_HEADS // 2, 2, 8, N_META))
    w2 = 2 * NA_DH
    qrows = NA_QROWS * GRID_W
    srows = NA_SLAB * GRID_W

    def slab_start(rb):
        start = N_META + GRID_W * jnp.clip(rb * NA_QROWS - NA_KH // 2, 0, GRID_H - NA_SLAB)
        return pl.multiple_of(start, N_META)

    def col(off):
        return lambda hp, rb: (pl.multiple_of(N_META + rb * qrows, N_META), pl.multiple_of(off + hp * w2, w2))

    def slab(off):
        return lambda hp, rb: (slab_start(rb), pl.multiple_of(off + hp * w2, w2))

    def meta(off):
        return lambda hp, rb: (0, off // w2 + hp)

    y_main = pl.pallas_call(
        _na_main_body,
        grid=(NA_HEADS // 2, GRID_H // NA_QROWS),
        in_specs=[
            pl.BlockSpec((pl.Element(qrows), pl.Element(w2)), col(OFF_Q)),
            pl.BlockSpec((pl.Element(srows), pl.Element(w2)), slab(OFF_K)),
            pl.BlockSpec((pl.Element(srows), pl.Element(w2)), slab(OFF_V)),
            pl.BlockSpec((N_META, w2), meta(OFF_K)),
            pl.BlockSpec((N_META, w2), meta(OFF_V)),
            pl.BlockSpec((None, 2, NA_KH, GRID_W, NA_KH * GRID_W), lambda hp, rb: (hp, 0, 0, 0, 0)),
            pl.BlockSpec((None, 2, 8, N_META), lambda hp, rb: (hp, 0, 0, 0)),
            pl.BlockSpec((pl.Element(qrows), pl.Element(w2)), col(OFF_NA_GATE)),
        ],
        out_specs=pl.BlockSpec((qrows, w2), lambda hp, rb: (rb, hp)),
        out_shape=jax.ShapeDtypeStruct((SEQ, C_BR), BF16),
        compiler_params=_cparams(("parallel", "parallel")),
        name="na_main",
    )(z, z, z, z, z, bt, mb, z)

    def mcol(off):
        return lambda hp: (0, off // w2 + hp)

    y_meta = pl.pallas_call(
        _na_meta_body,
        grid=(NA_HEADS // 2,),
        in_specs=[
            pl.BlockSpec((N_META, w2), mcol(OFF_Q)),
            pl.BlockSpec((N_META, w2), mcol(OFF_K)),
            pl.BlockSpec((N_META, w2), mcol(OFF_V)),
            pl.BlockSpec((None, 2, 8, N_META), lambda hp: (hp, 0, 0, 0)),
            pl.BlockSpec((N_META, w2), mcol(OFF_NA_GATE)),
        ],
        out_specs=pl.BlockSpec((N_META, w2), lambda hp: (0, hp)),
        out_shape=jax.ShapeDtypeStruct((N_META, C_BR), BF16),
        compiler_params=_cparams(("parallel",)),
        name="na_meta",
    )(z, z, z, mb, z)
    return y_meta, y_main


def _silu(g):
    return g * jax.nn.sigmoid(g)


def _merge_body(ya_ref, yb_ref, yc_ref, hg_ref, fg_ref, wa_ref, wb_ref, wc_ref, ga_ref, gb_ref, gc_ref, o_ref):
    dot = lambda a, b: jnp.dot(a.astype(BF16), b[...], preferred_element_type=F32)
    m = jax.nn.sigmoid(ga_ref[...]) * dot(ya_ref[...] * _silu(hg_ref[...]), wa_ref)
    m = m + jax.nn.sigmoid(gb_ref[...]) * dot(yb_ref[...] * _silu(fg_ref[...]), wb_ref)
    m = m + jax.nn.sigmoid(gc_ref[...]) * dot(yc_ref[...], wc_ref)
    o_ref[...] = m.astype(BF16)


def _merge(ya, yb, yc, wa, wb, wc, z, tn=512):
    n_rows = ya.shape[0]
    yspec = pl.BlockSpec((ROW_TILE, C_BR), lambda i, j: (i, 0))
    wspec = pl.BlockSpec((C_BR, tn), lambda i, j: (0, j))
    zspec = lambda off: pl.BlockSpec((ROW_TILE, C_BR), lambda i, j, off=off: (i, off // C_BR))
    gspec = lambda k: pl.BlockSpec((ROW_TILE, tn), lambda i, j, k=k: (i, (OFF_MERGE + k * D_MODEL) // tn + j))
    return pl.pallas_call(
        _merge_body,
        grid=(n_rows // ROW_TILE, D_MODEL // tn),
        in_specs=[yspec, yspec, yspec, zspec(OFF_HY_GATE), zspec(OFF_FN_GATE), wspec, wspec, wspec,
                  gspec(0), gspec(1), gspec(2)],
        out_specs=pl.BlockSpec((ROW_TILE, tn), lambda i, j: (i, j)),
        out_shape=jax.ShapeDtypeStruct((n_rows, D_MODEL), BF16),
        compiler_params=_cparams(("parallel", "parallel")),
        name="merge",
    )(ya, yb, yc, z, z, wa, wb, wc, z, z, z)


def _outproj_body(h_ref, m_ref, w_ref, o_ref):
    o_ref[...] = h_ref[...] + jnp.dot(m_ref[...], w_ref[...], preferred_element_type=F32)


def _outproj(h, m, w, tn=1024):
    n_rows = h.shape[0]
    return pl.pallas_call(
        _outproj_body,
        grid=(n_rows // ROW_TILE, D_MODEL // tn),
        in_specs=[
            pl.BlockSpec((ROW_TILE, tn), lambda i, j: (i, j)),
            pl.BlockSpec((ROW_TILE, D_MODEL), lambda i, j: (i, 0)),
            pl.BlockSpec((D_MODEL, tn), lambda i, j: (0, j)),
        ],
        out_specs=pl.BlockSpec((ROW_TILE, tn), lambda i, j: (i, j)),
        out_shape=jax.ShapeDtypeStruct((n_rows, D_MODEL), F32),
        compiler_params=_cparams(("parallel", "parallel")),
        name="outproj",
    )(h, m, w)


def _final_norm_body(x_ref, g_ref, o_ref):
    x = x_ref[...]
    y = x * lax.rsqrt(jnp.mean(x * x, axis=-1, keepdims=True) + EPS)
    o_ref[...] = y * g_ref[...]


def _final_norm(h, g, tm=512):
    return pl.pallas_call(
        _final_norm_body,
        grid=(SEQ // tm,),
        in_specs=[
            pl.BlockSpec((pl.Element(tm), pl.Element(D_MODEL)), lambda i: (pl.multiple_of(N_META + i * tm, N_META), 0)),
            pl.BlockSpec((1, D_MODEL), lambda i: (0, 0)),
        ],
        out_specs=pl.BlockSpec((tm, D_MODEL), lambda i: (i, 0)),
        out_shape=jax.ShapeDtypeStruct((SEQ, D_MODEL), F32),
        compiler_params=_cparams(("parallel",)),
        name="final_norm",
    )(h, g.reshape(1, D_MODEL))


def _hyena_branch(z, conv_w, conv_b, w1, b1, w2, b2, w3, b3, w4, freq, decay, skip):
    tab = _hyena_tables()
    fwd = [[_bf(tab["fwd_r"])], [_bf(tab["fwd_i"])]]
    inv_r, inv_i = _bf(tab["inv_r"]), _bf(tab["inv_i"])
    cb, sb = _bf(tab["cb"]), _bf(tab["sb"])

    hyc = _shortconv(z, conv_w, conv_b)
    filt = _hyena_filters(w1, b1, w2, b2, w3, b3, w4, freq, decay)
    ftr, fti = _stage_a([filt], [0], filt.shape[1], fwd)
    kr, ki = _filter_spectrum(ftr, fti, cb, sb)

    tr, ti = _stage_a([hyc], [0], C_BR, fwd)
    tr, ti = _conv_stage_b(tr, ti, kr, ki, 0, cb, sb)
    zmid = _stage_ainv(tr, ti, inv_r, inv_i, hyc, 0, hyc, C_BR, skip[0])

    tr, ti = _stage_a([zmid], [0], C_BR, fwd)
    tr, ti = _conv_stage_b(tr, ti, kr, ki, 1, cb, sb)
    return _stage_ainv(tr, ti, inv_r, inv_i, zmid, 0, hyc, 2 * C_BR, skip[1])


def _fnet_branch(z):
    tab = _fnet_tables()
    cs, sn = _bf(tab["cs"]), _bf(tab["sn"])
    p, q = _chan_dft(z, OFF_FN_IN, _bf(tab["chan"]))
    tr, ti = _stage_a([p, q], [0, 0], C_BR, [[cs, -sn], [-sn, -cs]])
    return _fnet_stage_b(tr, ti, _bf(tab["cb"]), _bf(tab["sb"]))


def _layer(h, norm_g, w_in, conv_w, conv_b, w1, b1, w2, b2, w3, b3, w4, freq, decay, skip, rpb, meta_bias,
           w_a, w_b, w_c, w_out):
    z = _inproj(h, norm_g, w_in.astype(BF16))
    ya = _hyena_branch(z, conv_w, conv_b, w1, b1, w2, b2, w3, b3, w4, freq, decay, skip)
    yb = _fnet_branch(z)
    yc_meta, yc_main = _neighborhood_attention(z, rpb, meta_bias)
    yc = jnp.concatenate([yc_meta, yc_main], axis=0)
    m = _merge(ya, yb, yc, w_a.astype(BF16), w_b.astype(BF16), w_c.astype(BF16), z)
    return _outproj(h, m, w_out.astype(BF16))


def kernel(x, meta_tokens, norm_g, w_in, hy_conv_w, hy_conv_b, hy_flt_w1, hy_flt_b1, hy_flt_w2, hy_flt_b2,
           hy_flt_w3, hy_flt_b3, hy_flt_w4, hy_flt_freq, hy_decay, hy_skip, na_rpb, na_meta_bias,
           w_branch_a, w_branch_b, w_branch_c, w_out, final_g):
    assert x.shape == (1, SEQ, D_MODEL)
    h = jnp.concatenate([meta_tokens.astype(x.dtype), x[0]], axis=0)
    for i in range(norm_g.shape[0]):
        h = _layer(h, norm_g[i], w_in[i], hy_conv_w[i], hy_conv_b[i], hy_flt_w1[i], hy_flt_b1[i],
                   hy_flt_w2[i], hy_flt_b2[i], hy_flt_w3[i], hy_flt_b3[i], hy_flt_w4[i], hy_flt_freq[i],
                   hy_decay[i].reshape(-1), hy_skip[i], na_rpb[i], na_meta_bias[i],
                   w_branch_a[i], w_branch_b[i], w_branch_c[i], w_out[i])
    return _final_norm(h, final_g)[None]
```

```python
import functools
import math

import numpy as np
import jax
import jax.numpy as jnp
from jax import lax
from jax.experimental import pallas as pl
from jax.experimental.pallas import tpu as pltpu

F32 = jnp.float32
BF16 = jnp.bfloat16

D_MODEL = 2048
SEQ = 16384
N_META = 16
SEQ_T = SEQ + N_META
GRID_W = 64
GRID_H = SEQ // GRID_W
C_BR = 1024
HY_EMB = 33
HY_HID = 64
NA_HEADS = 16
NA_DH = 64
NA_KH = 8
NA_KW = 16
EPS = 1e-6
NEG_INF = -1e30

OFF_HY_IN = 0
OFF_HY_GATE = 3072
OFF_FN_IN = 4096
OFF_FN_GATE = 5120
OFF_Q = 6144
OFF_K = 7168
OFF_V = 8192
OFF_NA_GATE = 9216
OFF_MERGE = 10240
N_IN = 16384

FA = 80
FB = 205
FBP = 208
BBLK = 8
KA_HY = FA + 1
KAP_HY = 96

ROW_TILE = 656
LANE = 128
VMEM_LIMIT = 48 * 1024 * 1024


def _cparams(sem):
    return pltpu.CompilerParams(dimension_semantics=sem, vmem_limit_bytes=VMEM_LIMIT)


@functools.lru_cache(maxsize=None)
def _hyena_tables():
    n_circ = 2 * SEQ_T
    a = np.arange(FA)[None, None, :]
    b = np.arange(FBP)[:, None, None]
    ka = np.arange(KAP_HY)[None, :, None]
    n = FB * a + b
    ang = 2.0 * np.pi * ((ka * n) % n_circ) / n_circ
    valid = (b < FB) & (ka < KA_HY)
    fwd_r = np.where(valid, np.cos(ang), 0.0)
    fwd_i = np.where(valid, -np.sin(ang), 0.0)
    c = np.where((ka == 0) | (ka == FA), 1.0, 2.0) / n_circ
    inv_r = np.transpose(np.where(valid, c * np.cos(ang), 0.0), (0, 2, 1))
    inv_i = np.transpose(np.where(valid, -c * np.sin(ang), 0.0), (0, 2, 1))
    kb = np.arange(FBP)[:, None]
    bb = np.arange(FBP)[None, :]
    phi = 2.0 * np.pi * ((kb * bb) % FB) / FB
    ok = (kb < FB) & (bb < FB)
    cb = np.where(ok, np.cos(phi), 0.0)
    sb = np.where(ok, np.sin(phi), 0.0)
    return dict(fwd_r=fwd_r, fwd_i=fwd_i, inv_r=inv_r, inv_i=inv_i, cb=cb, sb=sb)


@functools.lru_cache(maxsize=None)
def _fnet_tables():
    a = np.arange(FA)[None, None, :]
    b = np.arange(FBP)[:, None, None]
    ka = np.arange(FA)[None, :, None]
    n = FB * a + b
    ang = 2.0 * np.pi * ((ka * n) % SEQ_T) / SEQ_T
    valid = np.broadcast_to(b < FB, ang.shape)
    cs = np.where(valid, np.cos(ang), 0.0)
    sn = np.where(valid, np.sin(ang), 0.0)
    kb = np.arange(FBP)[:, None]
    bb = np.arange(FBP)[None, :]
    phi = 2.0 * np.pi * ((kb * bb) % FB) / FB
    ok = (kb < FB) & (bb < FB)
    scale = 1.0 / math.sqrt(SEQ_T * 256.0)
    cb = np.where(ok, np.cos(phi), 0.0) * scale
    sb = np.where(ok, np.sin(phi), 0.0) * scale
    j = np.arange(256)
    th = 2.0 * np.pi * ((j[:, None] * j[None, :]) % 256) / 256.0
    chan = np.concatenate([np.cos(th), np.sin(th)], axis=1)
    return dict(cs=cs, sn=sn, cb=cb, sb=sb, chan=chan)


@functools.lru_cache(maxsize=None)
def _filter_features():
    t = np.linspace(0.0, 1.0, SEQ_T)[:, None]
    bands = (HY_EMB - 1) // 2
    w = 2.0 * np.pi * np.arange(SEQ_T)[:, None] / SEQ_T
    f = np.linspace(1e-4, bands - 1, bands)[None, :]
    z = np.concatenate([t, np.cos(f * w), -np.sin(f * w)], axis=-1)
    out = np.zeros((SEQ_T, HY_HID), np.float64)
    out[:, :HY_EMB] = z
    return out


@functools.lru_cache(maxsize=None)
def _na_index_tables():
    qc = np.arange(GRID_W)[:, None]
    kc = np.arange(GRID_W)[None, :]
    cs = np.clip(qc - NA_KW // 2, 0, GRID_W - NA_KW)
    colmask = (kc >= cs) & (kc < cs + NA_KW)
    dc = np.clip(kc - qc + NA_KW - 1, 0, 2 * NA_KW - 2)
    return colmask, dc


def _bf(x):
    return jnp.asarray(np.asarray(x, np.float32), dtype=BF16)


def _inproj_body(x_ref, g_ref, w_ref, o_ref, xn_ref):
    @pl.when(pl.program_id(1) == 0)
    def _():
        x = x_ref[...]
        y = x * lax.rsqrt(jnp.mean(x * x, axis=-1, keepdims=True) + EPS)
        xn_ref[...] = (y * g_ref[...]).astype(BF16)

    o_ref[...] = jnp.dot(xn_ref[...], w_ref[...], preferred_element_type=F32)


def _inproj(h, g, w_bf16, tn=1024):
    n_rows, d = h.shape
    n_out = w_bf16.shape[1]
    return pl.pallas_call(
        _inproj_body,
        grid=(n_rows // ROW_TILE, n_out // tn),
        in_specs=[
            pl.BlockSpec((ROW_TILE, d), lambda i, j: (i, 0)),
            pl.BlockSpec((1, d), lambda i, j: (0, 0)),
            pl.BlockSpec((d, tn), lambda i, j: (0, j)),
        ],
        out_specs=pl.BlockSpec((ROW_TILE, tn), lambda i, j: (i, j)),
        out_shape=jax.ShapeDtypeStruct((n_rows, n_out), F32),
        scratch_shapes=[pltpu.VMEM((ROW_TILE, d), BF16)],
        compiler_params=_cparams(("parallel", "arbitrary")),
        name="inproj",
    )(h, g.reshape(1, d), w_bf16)


def _shortconv_body(prev_ref, cur_ref, next_ref, w_ref, b_ref, o_ref):
    i = pl.program_id(0)
    last = pl.num_programs(0) - 1
    x = cur_ref[...]
    rows = lax.broadcasted_iota(jnp.int32, x.shape, 0)
    prev_row = jnp.where(i == 0, 0.0, prev_ref[7:8, :])
    next_row = jnp.where(i == last, 0.0, next_ref[0:1, :])
    up = jnp.where(rows == 0, prev_row, pltpu.roll(x, 1, 0))
    dn = jnp.where(rows == ROW_TILE - 1, next_row, pltpu.roll(x, ROW_TILE - 1, 0))
    y = up * w_ref[0:1, :]
    y = y + x * w_ref[1:2, :]
    y = y + dn * w_ref[2:3, :]
    o_ref[...] = y + b_ref[...]


def _shortconv(z, w, b, tc=1024):
    n_rows = z.shape[0]
    n_c = w.shape[1]
    n8 = n_rows // 8
    r8 = ROW_TILE // 8
    return pl.pallas_call(
        _shortconv_body,
        grid=(n_rows // ROW_TILE, n_c // tc),
        in_specs=[
            pl.BlockSpec((8, tc), lambda i, j: (jnp.maximum(i * r8 - 1, 0), j)),
            pl.BlockSpec((ROW_TILE, tc), lambda i, j: (i, j)),
            pl.BlockSpec((8, tc), lambda i, j: (jnp.minimum((i + 1) * r8, n8 - 1), j)),
            pl.BlockSpec((3, tc), lambda i, j: (0, j)),
            pl.BlockSpec((1, tc), lambda i, j: (0, j)),
        ],
        out_specs=pl.BlockSpec((ROW_TILE, tc), lambda i, j: (i, j)),
        out_shape=jax.ShapeDtypeStruct((n_rows, n_c), F32),
        compiler_params=_cparams(("parallel", "parallel")),
        name="shortconv",
    )(z, z, z, w, b.reshape(1, n_c))


def _hp_dot(a, b):
    return jnp.dot(a, b, preferred_element_type=F32, precision=lax.Precision.HIGHEST)


def _filter_body(z_ref, w1_ref, b1_ref, w2_ref, b2_ref, w3_ref, b3_ref, fr_ref, w4_ref, dec_ref, o_ref, h_ref):
    i = pl.program_id(0)
    j = pl.program_id(1)

    @pl.when(j == 0)
    def _():
        fr = fr_ref[...]
        h = jnp.sin(fr * (_hp_dot(z_ref[...], w1_ref[...]) + b1_ref[...]))
        h = jnp.sin(fr * (_hp_dot(h, w2_ref[...]) + b2_ref[...]))
        h_ref[...] = jnp.sin(fr * (_hp_dot(h, w3_ref[...]) + b3_ref[...]))

    t = z_ref[:, 0:1]
    y = _hp_dot(h_ref[...], w4_ref[...]) * jnp.exp(-t * jnp.abs(dec_ref[...]))
    rows = lax.broadcasted_iota(jnp.int32, y.shape, 0)
    drop = jnp.logical_and(jnp.logical_and(i == 0, j % 2 == 1), rows == 0)
    o_ref[...] = jnp.where(drop, 0.0, y)


def _hyena_filters(w1, b1, w2, b2, w3, b3, w4, freq, decay):
    zfeat = jnp.asarray(_filter_features(), dtype=F32)
    w1p = jnp.zeros((HY_HID, HY_HID), F32).at[:HY_EMB].set(w1)
    n_c = w4.shape[1]
    tc = C_BR
    small = lambda i, j: (0, 0)
    return pl.pallas_call(
        _filter_body,
        grid=(SEQ_T // ROW_TILE, n_c // tc),
        in_specs=[
            pl.BlockSpec((ROW_TILE, HY_HID), lambda i, j: (i, 0)),
            pl.BlockSpec((HY_HID, HY_HID), small),
            pl.BlockSpec((1, HY_HID), small),
            pl.BlockSpec((HY_HID, HY_HID), small),
            pl.BlockSpec((1, HY_HID), small),
            pl.BlockSpec((HY_HID, HY_HID), small),
            pl.BlockSpec((1, HY_HID), small),
            pl.BlockSpec((1, HY_HID), small),
            pl.BlockSpec((HY_HID, tc), lambda i, j: (0, j)),
            pl.BlockSpec((1, tc), lambda i, j: (0, j)),
        ],
        out_specs=pl.BlockSpec((ROW_TILE, tc), lambda i, j: (i, j)),
        out_shape=jax.ShapeDtypeStruct((SEQ_T, n_c), F32),
        scratch_shapes=[pltpu.VMEM((ROW_TILE, HY_HID), F32)],
        compiler_params=_cparams(("parallel", "arbitrary")),
        name="hyena_filter",
    )(zfeat, w1p, b1.reshape(1, -1), w2, b2.reshape(1, -1), w3, b3.reshape(1, -1), freq.reshape(1, -1),
      w4, decay.reshape(1, n_c))


def _stage_a_body(n_in, n_out, *refs):
    x_refs = refs[:n_in]
    w_refs = refs[n_in:n_in + n_in * n_out]
    o_refs = refs[n_in + n_in * n_out:]
    bb = pl.program_id(0)
    for j in range(BBLK):
        valid = bb * BBLK + j < FB
        xs = [jnp.where(valid, xr[:, j, :], 0.0).astype(BF16) for xr in x_refs]
        for o in range(n_out):
            acc = None
            for k in range(n_in):
                d = jnp.dot(w_refs[o * n_in + k][j], xs[k], preferred_element_type=F32)
                acc = d if acc is None else acc + d
            o_refs[o][j] = acc.astype(BF16)


def _stage_a(xs, col_offs, n_cols, w_tabs, ct=1024):
    n_in = len(xs)
    n_out = len(w_tabs)
    m = w_tabs[0][0].shape[1]
    x3 = [x.reshape(FA, FB, x.shape[1]) for x in xs]
    in_specs = []
    for k in range(n_in):
        off = col_offs[k] // ct
        in_specs.append(pl.BlockSpec((FA, BBLK, ct), lambda b, c, off=off: (0, b, off + c)))
    flat_w = []
    for o in range(n_out):
        for k in range(n_in):
            flat_w.append(w_tabs[o][k])
            in_specs.append(pl.BlockSpec((BBLK, m, FA), lambda b, c: (b, 0, 0)))
    return pl.pallas_call(
        functools.partial(_stage_a_body, n_in, n_out),
        grid=(FBP // BBLK, n_cols // ct),
        in_specs=in_specs,
        out_specs=[pl.BlockSpec((BBLK, m, ct), lambda b, c: (b, 0, c)) for _ in range(n_out)],
        out_shape=[jax.ShapeDtypeStruct((FBP, m, n_cols), BF16) for _ in range(n_out)],
        compiler_params=_cparams(("parallel", "parallel")),
        name="dft_stage_a",
    )(*x3, *flat_w)


def _stage_ainv_body(tr_ref, ti_ref, wr_ref, wi_ref, v_ref, x_ref, skip_ref, o_ref):
    skip = skip_ref[...]
    for j in range(BBLK):
        y = jnp.dot(wr_ref[j], tr_ref[j], preferred_element_type=F32)
        y = y + jnp.dot(wi_ref[j], ti_ref[j], preferred_element_type=F32)
        y = y + v_ref[:, j, :] * skip
        o_ref[:, j, :] = x_ref[:, j, :] * y


def _stage_ainv(tr, ti, wr, wi, v, v_off, xmul, x_off, skip, ct=1024):
    n_cols = tr.shape[2]
    kap = tr.shape[1]
    view = lambda x: x.reshape(FA, FB, x.shape[1])
    spec3 = lambda off: pl.BlockSpec((FA, BBLK, ct), lambda b, c, off=off // ct: (0, b, off + c))
    in_specs = [
        pl.BlockSpec((BBLK, kap, ct), lambda b, c: (b, 0, c)),
        pl.BlockSpec((BBLK, kap, ct), lambda b, c: (b, 0, c)),
        pl.BlockSpec((BBLK, FA, kap), lambda b, c: (b, 0, 0)),
        pl.BlockSpec((BBLK, FA, kap), lambda b, c: (b, 0, 0)),
        spec3(v_off),
        spec3(x_off),
        pl.BlockSpec((1, ct), lambda b, c: (0, c)),
    ]
    args = [tr, ti, wr, wi, view(v), view(xmul), skip.reshape(1, n_cols)]
    out = pl.pallas_call(
        _stage_ainv_body,
        grid=(FBP // BBLK, n_cols // ct),
        in_specs=in_specs,
        out_specs=pl.BlockSpec((FA, BBLK, ct), lambda b, c: (0, b, c)),
        out_shape=jax.ShapeDtypeStruct((FA, FB, n_cols), F32),
        compiler_params=_cparams(("parallel", "parallel")),
        name="dft_stage_a_inv",
    )(*args)
    return out.reshape(SEQ_T, n_cols)


def _filter_spec_body(fr_ref, fi_ref, br_ref, bi_ref, c_ref, s_ref, kr_ref, ki_ref):
    c = c_ref[...]
    s = s_ref[...]
    dot = lambda a, b: jnp.dot(a, b[...], preferred_element_type=F32)
    kr_ref[...] = dot(c, fr_ref) + dot(s, fi_ref) + dot(c, br_ref) + dot(s, bi_ref)
    ki_ref[...] = dot(c, fi_ref) - dot(s, fr_ref) - dot(c, bi_ref) + dot(s, br_ref)


def _filter_spectrum(tr, ti, cb, sb, ct=1024):
    nc = C_BR // ct
    kap = tr.shape[1]
    n_cols = tr.shape[2]
    tr2 = tr.reshape(FBP, kap * n_cols)
    ti2 = ti.reshape(FBP, kap * n_cols)
    ncol_blk = n_cols // ct

    def tspec(direction):
        return pl.BlockSpec((FBP, ct), lambda o, ka, c: (0, ka * ncol_blk + (2 * o + direction) * nc + c))

    mat = pl.BlockSpec((FBP, FBP), lambda o, ka, c: (0, 0))
    ospec = pl.BlockSpec((None, None, FBP, ct), lambda o, ka, c: (o, ka, 0, c))
    oshape = jax.ShapeDtypeStruct((2, KA_HY, FBP, C_BR), F32)
    return pl.pallas_call(
        _filter_spec_body,
        grid=(2, KA_HY, nc),
        in_specs=[tspec(0), tspec(0), tspec(1), tspec(1), mat, mat],
        out_specs=[ospec, ospec],
        out_shape=[oshape, oshape],
        compiler_params=_cparams(("parallel", "parallel", "parallel")),
        name="filter_spectrum",
    )(tr2, ti2, tr2, ti2, cb, sb)


def _conv_b_body(tr_ref, ti_ref, kr_ref, ki_ref, c_ref, s_ref, or_ref, oi_ref):
    ka = pl.program_id(0)

    @pl.when(ka < KA_HY)
    def _():
        c = c_ref[...]
        s = s_ref[...]
        dot = lambda a, b: jnp.dot(a, b, preferred_element_type=F32)
        tr = tr_ref[...]
        ti = ti_ref[...]
        yr = dot(c, tr) + dot(s, ti)
        yi = dot(c, ti) - dot(s, tr)
        kr = kr_ref[...]
        ki = ki_ref[...]
        zr = (yr * kr - yi * ki).astype(BF16)
        zi = (yr * ki + yi * kr).astype(BF16)
        or_ref[...] = (dot(c, zr) - dot(s, zi)).astype(BF16)
        oi_ref[...] = (dot(c, zi) + dot(s, zr)).astype(BF16)

    @pl.when(ka >= KA_HY)
    def _():
        or_ref[...] = jnp.zeros_like(or_ref)
        oi_ref[...] = jnp.zeros_like(oi_ref)


def _conv_stage_b(tr, ti, kr, ki, order, cb, sb, ct=1024):
    kap = tr.shape[1]
    n_cols = tr.shape[2]
    nc = n_cols // ct
    tr2 = tr.reshape(FBP, kap * n_cols)
    ti2 = ti.reshape(FBP, kap * n_cols)
    tspec = pl.BlockSpec((FBP, ct), lambda ka, c: (0, ka * nc + c))
    kspec = pl.BlockSpec((None, None, FBP, ct), lambda ka, c: (order, jnp.minimum(ka, KA_HY - 1), 0, c))
    mat = pl.BlockSpec((FBP, FBP), lambda ka, c: (0, 0))
    oshape = jax.ShapeDtypeStruct((FBP, kap * n_cols), BF16)
    o_r, o_i = pl.pallas_call(
        _conv_b_body,
        grid=(kap, nc),
        in_specs=[tspec, tspec, kspec, kspec, mat, mat],
        out_specs=[tspec, tspec],
        out_shape=[oshape, oshape],
        compiler_params=_cparams(("parallel", "parallel")),
        name="conv_stage_b",
    )(tr2, ti2, kr, ki, cb, sb)
    return o_r.reshape(FBP, kap, n_cols), o_i.reshape(FBP, kap, n_cols)


def _fnet_b_body(tr_ref, ti_ref, c_ref, s_ref, o_ref):
    dot = lambda a, b: jnp.dot(a, b, preferred_element_type=F32)
    y = dot(c_ref[...], tr_ref[...]) + dot(s_ref[...], ti_ref[...])
    o_ref[...] = y[:FB, :]


def _fnet_stage_b(tr, ti, cb, sb, ct=1024):
    n_cols = tr.shape[2]
    nc = n_cols // ct
    tr2 = tr.reshape(FBP, FA * n_cols)
    ti2 = ti.reshape(FBP, FA * n_cols)
    tspec = pl.BlockSpec((FBP, ct), lambda ka, c: (0, ka * nc + c))
    mat = pl.BlockSpec((FBP, FBP), lambda ka, c: (0, 0))
    out = pl.pallas_call(
        _fnet_b_body,
        grid=(FA, nc),
        in_specs=[tspec, tspec, mat, mat],
        out_specs=pl.BlockSpec((FB, ct), lambda ka, c: (0, ka * nc + c)),
        out_shape=jax.ShapeDtypeStruct((FB, FA * n_cols), F32),
        compiler_params=_cparams(("parallel", "parallel")),
        name="fnet_stage_b",
    )(tr2, ti2, cb, sb)
    return out.reshape(SEQ_T, n_cols)


def _chan_dft_body(x_ref, w_ref, p_ref, q_ref):
    y = jnp.dot(x_ref[...].astype(BF16), w_ref[...], preferred_element_type=F32)
    p_ref[...] = y[:, :256]
    q_ref[...] = y[:, 256:]


def _chan_dft(z, off, chan):
    oshape = jax.ShapeDtypeStruct((SEQ_T, C_BR), F32)
    return pl.pallas_call(
        _chan_dft_body,
        grid=(SEQ_T // ROW_TILE, 4),
        in_specs=[
            pl.BlockSpec((ROW_TILE, 256), lambda i, g: (i, off // 256 + g)),
            pl.BlockSpec((256, 512), lambda i, g: (0, 0)),
        ],
        out_specs=[pl.BlockSpec((ROW_TILE, 256), lambda i, g: (i, g))] * 2,
        out_shape=[oshape, oshape],
        compiler_params=_cparams(("parallel", "parallel")),
        name="fnet_chan_dft",
    )(z, chan)


NA_QROWS = 8
NA_SLAB = 16
NA_SCALE = NA_DH ** -0.5


def _na_main_body(q_ref, k_ref, v_ref, km_ref, vm_ref, bt_ref, mb_ref, g_ref, o_ref):
    rb = pl.program_id(1)
    slab0 = jnp.clip(rb * NA_QROWS - NA_KH // 2, 0, GRID_H - NA_SLAB)
    w2 = 2 * NA_DH
    lane = lax.broadcasted_iota(jnp.int32, (GRID_W, w2), 1)
    row2 = lax.broadcasted_iota(jnp.int32, (w2, w2), 0)
    lane2 = lax.broadcasted_iota(jnp.int32, (w2, w2), 1)
    own_head = (row2 >= NA_DH) == (lane2 >= NA_DH)
    km = km_ref[...].astype(BF16)
    vm = vm_ref[...].astype(BF16)
    mbt = mb_ref[...]
    dn_t = (((1,), (1,)), ((), ()))
    dn_k = (((0,), (0,)), ((), ()))
    for i in range(NA_QROWS):
        r = rb * NA_QROWS + i
        r0 = jnp.clip(r - NA_KH // 2, 0, GRID_H - NA_KH)
        off = pl.multiple_of((r0 - slab0) * GRID_W, GRID_W)
        d0 = r0 - r + (NA_KH - 1)
        q = q_ref[i * GRID_W:(i + 1) * GRID_W, :]
        q2 = jnp.where(own_head, jnp.concatenate([q, q], axis=0), 0.0).astype(BF16)
        ks = k_ref[pl.ds(off, NA_KH * GRID_W), :].astype(BF16)
        vs = v_ref[pl.ds(off, NA_KH * GRID_W), :].astype(BF16)
        st = lax.dot_general(ks, q2, dn_t, preferred_element_type=F32) * NA_SCALE + bt_ref[d0]
        sx = lax.dot_general(km, q2, dn_t, preferred_element_type=F32) * NA_SCALE + mbt
        m = jnp.maximum(jnp.max(st, axis=0, keepdims=True), jnp.max(sx, axis=0, keepdims=True))
        p = jnp.exp(st - m)
        px = jnp.exp(sx - m)
        inv = 1.0 / (jnp.sum(p, axis=0, keepdims=True) + jnp.sum(px, axis=0, keepdims=True))
        o2 = lax.dot_general((p * inv).astype(BF16), vs, dn_k, preferred_element_type=F32)
        o2 = o2 + lax.dot_general((px * inv).astype(BF16), vm, dn_k, preferred_element_type=F32)
        o = jnp.where(lane < NA_DH, o2[:GRID_W], o2[GRID_W:])
        g = g_ref[i * GRID_W:(i + 1) * GRID_W, :]
        o_ref[i * GRID_W:(i + 1) * GRID_W, :] = (o * (g * jax.nn.sigmoid(g))).astype(BF16)


def _na_meta_body(q_ref, k_ref, v_ref, mb_ref, g_ref, o_ref):
    lane = lax.broadcasted_iota(jnp.int32, (N_META, 2 * NA_DH), 1)
    q = q_ref[...]
    km = k_ref[...].astype(BF16)
    vm = v_ref[...].astype(BF16)
    dn_t = (((1,), (1,)), ((), ()))
    outs = []
    for hh in range(2):
        sel = (lane >= hh * NA_DH) & (lane < (hh + 1) * NA_DH)
        qm = jnp.where(sel, q, 0.0).astype(BF16)
        s = lax.dot_general(qm, km, dn_t, preferred_element_type=F32) * NA_SCALE + mb_ref[hh, 0:1, :]
        m = jnp.max(s, axis=-1, keepdims=True)
        p = jnp.exp(s - m)
        den = jnp.sum(p, axis=-1, keepdims=True)
        outs.append(jnp.dot(p.astype(BF16), vm, preferred_element_type=F32) / den)
    o = jnp.where(lane < NA_DH, outs[0], outs[1])
    g = g_ref[...]
    o_ref[...] = (o * (g * jax.nn.sigmoid(g))).astype(BF16)


def _na_bias_table(rpb):
    colmask, dc = _na_index_tables()
    rows = np.arange(NA_KH)[:, None] + np.arange(NA_KH)[None, :]
    t = rpb.astype(F32)[:, rows]
    t = t[:, :, :, dc]
    t = jnp.where(colmask[None, None, None], t, NEG_INF)
    t = t.reshape(NA_HEADS // 2, 2, NA_KH, NA_KH, GRID_W, GRID_W)
    t = jnp.transpose(t, (0, 2, 3, 5, 1, 4))
    return t.reshape(NA_HEADS // 2, NA_KH, NA_KH * GRID_W, 2 * GRID_W)


def _neighborhood_attention(z, rpb, meta_bias):
    bt = _na_bias_table(rpb)
    mb = jnp.broadcast_to(meta_bias.astype(F32).reshape(NA_HEADS // 2, 2, 1, N_META),
                          (NA_HEADS // 2, 2, 8, N_META))
    mbt = jnp.transpose(meta_bias.astype(F32).reshape(NA_HEADS // 2, 2, N_META), (0, 2, 1))
    mbt = jnp.repeat(mbt, GRID_W, axis=2)
    w2 = 2 * NA_DH
    qrows = NA_QROWS * GRID_W
    srows = NA_SLAB * GRID_W

    def slab_start(rb):
        start = N_META + GRID_W * jnp.clip(rb * NA_QROWS - NA_KH // 2, 0, GRID_H - NA_SLAB)
        return pl.multiple_of(start, N_META)

    def col(off):
        return lambda hp, rb: (pl.multiple_of(N_META + rb * qrows, N_META), pl.multiple_of(off + hp * w2, w2))

    def slab(off):
        return lambda hp, rb: (slab_start(rb), pl.multiple_of(off + hp * w2, w2))

    def meta(off):
        return lambda hp, rb: (0, off // w2 + hp)

    y_main = pl.pallas_call(
        _na_main_body,
        grid=(NA_HEADS // 2, GRID_H // NA_QROWS),
        in_specs=[
            pl.BlockSpec((pl.Element(qrows), pl.Element(w2)), col(OFF_Q)),
            pl.BlockSpec((pl.Element(srows), pl.Element(w2)), slab(OFF_K)),
            pl.BlockSpec((pl.Element(srows), pl.Element(w2)), slab(OFF_V)),
            pl.BlockSpec((N_META, w2), meta(OFF_K)),
            pl.BlockSpec((N_META, w2), meta(OFF_V)),
            pl.BlockSpec((None, NA_KH, NA_KH * GRID_W, w2), lambda hp, rb: (hp, 0, 0, 0)),
            pl.BlockSpec((None, N_META, w2), lambda hp, rb: (hp, 0, 0)),
            pl.BlockSpec((pl.Element(qrows), pl.Element(w2)), col(OFF_NA_GATE)),
        ],
        out_specs=pl.BlockSpec((qrows, w2), lambda hp, rb: (rb, hp)),
        out_shape=jax.ShapeDtypeStruct((SEQ, C_BR), BF16),
        compiler_params=_cparams(("parallel", "parallel")),
        name="na_main",
    )(z, z, z, z, z, bt, mbt, z)

    def mcol(off):
        return lambda hp: (0, off // w2 + hp)

    y_meta = pl.pallas_call(
        _na_meta_body,
        grid=(NA_HEADS // 2,),
        in_specs=[
            pl.BlockSpec((N_META, w2), mcol(OFF_Q)),
            pl.BlockSpec((N_META, w2), mcol(OFF_K)),
            pl.BlockSpec((N_META, w2), mcol(OFF_V)),
            pl.BlockSpec((None, 2, 8, N_META), lambda hp: (hp, 0, 0, 0)),
            pl.BlockSpec((N_META, w2), mcol(OFF_NA_GATE)),
        ],
        out_specs=pl.BlockSpec((N_META, w2), lambda hp: (0, hp)),
        out_shape=jax.ShapeDtypeStruct((N_META, C_BR), BF16),
        compiler_params=_cparams(("parallel",)),
        name="na_meta",
    )(z, z, z, mb, z)
    return y_meta, y_main


def _silu(g):
    return g * jax.nn.sigmoid(g)


def _merge_body(ya_ref, yb_ref, yc_ref, hg_ref, fg_ref, wa_ref, wb_ref, wc_ref, ga_ref, gb_ref, gc_ref, o_ref):
    dot = lambda a, b: jnp.dot(a.astype(BF16), b[...], preferred_element_type=F32)
    m = jax.nn.sigmoid(ga_ref[...]) * dot(ya_ref[...] * _silu(hg_ref[...]), wa_ref)
    m = m + jax.nn.sigmoid(gb_ref[...]) * dot(yb_ref[...] * _silu(fg_ref[...]), wb_ref)
    m = m + jax.nn.sigmoid(gc_ref[...]) * dot(yc_ref[...], wc_ref)
    o_ref[...] = m.astype(BF16)


def _merge(ya, yb, yc, wa, wb, wc, z, tn=512):
    n_rows = ya.shape[0]
    yspec = pl.BlockSpec((ROW_TILE, C_BR), lambda i, j: (i, 0))
    wspec = pl.BlockSpec((C_BR, tn), lambda i, j: (0, j))
    zspec = lambda off: pl.BlockSpec((ROW_TILE, C_BR), lambda i, j, off=off: (i, off // C_BR))
    gspec = lambda k: pl.BlockSpec((ROW_TILE, tn), lambda i, j, k=k: (i, (OFF_MERGE + k * D_MODEL) // tn + j))
    return pl.pallas_call(
        _merge_body,
        grid=(n_rows // ROW_TILE, D_MODEL // tn),
        in_specs=[yspec, yspec, yspec, zspec(OFF_HY_GATE), zspec(OFF_FN_GATE), wspec, wspec, wspec,
                  gspec(0), gspec(1), gspec(2)],
        out_specs=pl.BlockSpec((ROW_TILE, tn), lambda i, j: (i, j)),
        out_shape=jax.ShapeDtypeStruct((n_rows, D_MODEL), BF16),
        compiler_params=_cparams(("parallel", "parallel")),
        name="merge",
    )(ya, yb, yc, z, z, wa, wb, wc, z, z, z)


def _outproj_body(h_ref, m_ref, w_ref, o_ref):
    o_ref[...] = h_ref[...] + jnp.dot(m_ref[...], w_ref[...], preferred_element_type=F32)


def _outproj(h, m, w, tn=1024):
    n_rows = h.shape[0]
    return pl.pallas_call(
        _outproj_body,
        grid=(n_rows // ROW_TILE, D_MODEL // tn),
        in_specs=[
            pl.BlockSpec((ROW_TILE, tn), lambda i, j: (i, j)),
            pl.BlockSpec((ROW_TILE, D_MODEL), lambda i, j: (i, 0)),
            pl.BlockSpec((D_MODEL, tn), lambda i, j: (0, j)),
        ],
        out_specs=pl.BlockSpec((ROW_TILE, tn), lambda i, j: (i, j)),
        out_shape=jax.ShapeDtypeStruct((n_rows, D_MODEL), F32),
        compiler_params=_cparams(("parallel", "parallel")),
        name="outproj",
    )(h, m, w)


def _final_norm_body(x_ref, g_ref, o_ref):
    x = x_ref[...]
    y = x * lax.rsqrt(jnp.mean(x * x, axis=-1, keepdims=True) + EPS)
    o_ref[...] = y * g_ref[...]


def _final_norm(h, g, tm=512):
    return pl.pallas_call(
        _final_norm_body,
        grid=(SEQ // tm,),
        in_specs=[
            pl.BlockSpec((pl.Element(tm), pl.Element(D_MODEL)), lambda i: (pl.multiple_of(N_META + i * tm, N_META), 0)),
            pl.BlockSpec((1, D_MODEL), lambda i: (0, 0)),
        ],
        out_specs=pl.BlockSpec((tm, D_MODEL), lambda i: (i, 0)),
        out_shape=jax.ShapeDtypeStruct((SEQ, D_MODEL), F32),
        compiler_params=_cparams(("parallel",)),
        name="final_norm",
    )(h, g.reshape(1, D_MODEL))


def _hyena_branch(z, conv_w, conv_b, w1, b1, w2, b2, w3, b3, w4, freq, decay, skip):
    tab = _hyena_tables()
    fwd = [[_bf(tab["fwd_r"])], [_bf(tab["fwd_i"])]]
    inv_r, inv_i = _bf(tab["inv_r"]), _bf(tab["inv_i"])
    cb, sb = _bf(tab["cb"]), _bf(tab["sb"])

    hyc = _shortconv(z, conv_w, conv_b)
    filt = _hyena_filters(w1, b1, w2, b2, w3, b3, w4, freq, decay)
    ftr, fti = _stage_a([filt], [0], filt.shape[1], fwd)
    kr, ki = _filter_spectrum(ftr, fti, cb, sb)

    tr, ti = _stage_a([hyc], [0], C_BR, fwd)
    tr, ti = _conv_stage_b(tr, ti, kr, ki, 0, cb, sb)
    zmid = _stage_ainv(tr, ti, inv_r, inv_i, hyc, 0, hyc, C_BR, skip[0])

    tr, ti = _stage_a([zmid], [0], C_BR, fwd)
    tr, ti = _conv_stage_b(tr, ti, kr, ki, 1, cb, sb)
    return _stage_ainv(tr, ti, inv_r, inv_i, zmid, 0, hyc, 2 * C_BR, skip[1])


def _fnet_branch(z):
    tab = _fnet_tables()
    cs, sn = _bf(tab["cs"]), _bf(tab["sn"])
    p, q = _chan_dft(z, OFF_FN_IN, _bf(tab["chan"]))
    tr, ti = _stage_a([p, q], [0, 0], C_BR, [[cs, -sn], [-sn, -cs]])
    return _fnet_stage_b(tr, ti, _bf(tab["cb"]), _bf(tab["sb"]))


def _layer(h, norm_g, w_in, conv_w, conv_b, w1, b1, w2, b2, w3, b3, w4, freq, decay, skip, rpb, meta_bias,
           w_a, w_b, w_c, w_out):
    z = _inproj(h, norm_g, w_in.astype(BF16))
    ya = _hyena_branch(z, conv_w, conv_b, w1, b1, w2, b2, w3, b3, w4, freq, decay, skip)
    yb = _fnet_branch(z)
    yc_meta, yc_main = _neighborhood_attention(z, rpb, meta_bias)
    yc = jnp.concatenate([yc_meta, yc_main], axis=0)
    m = _merge(ya, yb, yc, w_a.astype(BF16), w_b.astype(BF16), w_c.astype(BF16), z)
    return _outproj(h, m, w_out.astype(BF16))


def kernel(x, meta_tokens, norm_g, w_in, hy_conv_w, hy_conv_b, hy_flt_w1, hy_flt_b1, hy_flt_w2, hy_flt_b2,
           hy_flt_w3, hy_flt_b3, hy_flt_w4, hy_flt_freq, hy_decay, hy_skip, na_rpb, na_meta_bias,
           w_branch_a, w_branch_b, w_branch_c, w_out, final_g):
    assert x.shape == (1, SEQ, D_MODEL)
    h = jnp.concatenate([meta_tokens.astype(x.dtype), x[0]], axis=0)
    for i in range(norm_g.shape[0]):
        h = _layer(h, norm_g[i], w_in[i], hy_conv_w[i], hy_conv_b[i], hy_flt_w1[i], hy_flt_b1[i],
                   hy_flt_w2[i], hy_flt_b2[i], hy_flt_w3[i], hy_flt_b3[i], hy_flt_w4[i], hy_flt_freq[i],
                   hy_decay[i].reshape(-1), hy_skip[i], na_rpb[i], na_meta_bias[i],
                   w_branch_a[i], w_branch_b[i], w_branch_c[i], w_out[i])
    return _final_norm(h, final_g)[None]
```

```python
import functools
import math

import numpy as np
import jax
import jax.numpy as jnp
from jax import lax
from jax.experimental import pallas as pl
from jax.experimental.pallas import tpu as pltpu

F32 = jnp.float32
BF16 = jnp.bfloat16

D_MODEL = 2048
SEQ = 16384
N_META = 16
SEQ_T = SEQ + N_META
GRID_W = 64
GRID_H = SEQ // GRID_W
C_BR = 1024
HY_EMB = 33
HY_HID = 64
NA_HEADS = 16
NA_DH = 64
NA_KH = 8
NA_KW = 16
EPS = 1e-6
NEG_INF = -1e30

OFF_HY_IN = 0
OFF_HY_GATE = 3072
OFF_FN_IN = 4096
OFF_FN_GATE = 5120
OFF_Q = 6144
OFF_K = 7168
OFF_V = 8192
OFF_NA_GATE = 9216
OFF_MERGE = 10240
N_IN = 16384

FA = 80
FB = 205
FBP = 208
BBLK = 8
KA_HY = FA + 1
KAP_HY = 96

ROW_TILE = 656
LANE = 128
VMEM_LIMIT = 48 * 1024 * 1024


def _cparams(sem):
    return pltpu.CompilerParams(dimension_semantics=sem, vmem_limit_bytes=VMEM_LIMIT)


@functools.lru_cache(maxsize=None)
def _hyena_tables():
    n_circ = 2 * SEQ_T
    a = np.arange(FA)[None, None, :]
    b = np.arange(FBP)[:, None, None]
    ka = np.arange(KAP_HY)[None, :, None]
    n = FB * a + b
    ang = 2.0 * np.pi * ((ka * n) % n_circ) / n_circ
    valid = (b < FB) & (ka < KA_HY)
    fwd_r = np.where(valid, np.cos(ang), 0.0)
    fwd_i = np.where(valid, -np.sin(ang), 0.0)
    c = np.where((ka == 0) | (ka == FA), 1.0, 2.0) / n_circ
    inv_r = np.transpose(np.where(valid, c * np.cos(ang), 0.0), (0, 2, 1))
    inv_i = np.transpose(np.where(valid, -c * np.sin(ang), 0.0), (0, 2, 1))
    kb = np.arange(FBP)[:, None]
    bb = np.arange(FBP)[None, :]
    phi = 2.0 * np.pi * ((kb * bb) % FB) / FB
    ok = (kb < FB) & (bb < FB)
    cb = np.where(ok, np.cos(phi), 0.0)
    sb = np.where(ok, np.sin(phi), 0.0)
    return dict(fwd_r=fwd_r, fwd_i=fwd_i, inv_r=inv_r, inv_i=inv_i, cb=cb, sb=sb)


@functools.lru_cache(maxsize=None)
def _fnet_tables():
    a = np.arange(FA)[None, None, :]
    b = np.arange(FBP)[:, None, None]
    ka = np.arange(FA)[None, :, None]
    n = FB * a + b
    ang = 2.0 * np.pi * ((ka * n) % SEQ_T) / SEQ_T
    valid = np.broadcast_to(b < FB, ang.shape)
    cs = np.where(valid, np.cos(ang), 0.0)
    sn = np.where(valid, np.sin(ang), 0.0)
    kb = np.arange(FBP)[:, None]
    bb = np.arange(FBP)[None, :]
    phi = 2.0 * np.pi * ((kb * bb) % FB) / FB
    ok = (kb < FB) & (bb < FB)
    scale = 1.0 / math.sqrt(SEQ_T * 256.0)
    cb = np.where(ok, np.cos(phi), 0.0) * scale
    sb = np.where(ok, np.sin(phi), 0.0) * scale
    j = np.arange(256)
    th = 2.0 * np.pi * ((j[:, None] * j[None, :]) % 256) / 256.0
    chan = np.concatenate([np.cos(th), np.sin(th)], axis=1)
    return dict(cs=cs, sn=sn, cb=cb, sb=sb, chan=chan)


@functools.lru_cache(maxsize=None)
def _filter_features():
    t = np.linspace(0.0, 1.0, SEQ_T)[:, None]
    bands = (HY_EMB - 1) // 2
    w = 2.0 * np.pi * np.arange(SEQ_T)[:, None] / SEQ_T
    f = np.linspace(1e-4, bands - 1, bands)[None, :]
    z = np.concatenate([t, np.cos(f * w), -np.sin(f * w)], axis=-1)
    out = np.zeros((SEQ_T, HY_HID), np.float64)
    out[:, :HY_EMB] = z
    return out


@functools.lru_cache(maxsize=None)
def _na_index_tables():
    qc = np.arange(GRID_W)[:, None]
    kc = np.arange(GRID_W)[None, :]
    cs = np.clip(qc - NA_KW // 2, 0, GRID_W - NA_KW)
    colmask = (kc >= cs) & (kc < cs + NA_KW)
    dc = np.clip(kc - qc + NA_KW - 1, 0, 2 * NA_KW - 2)
    return colmask, dc


def _bf(x):
    return jnp.asarray(np.asarray(x, np.float32), dtype=BF16)


def _inproj_body(x_ref, g_ref, w_ref, o_ref, xn_ref):
    @pl.when(pl.program_id(1) == 0)
    def _():
        x = x_ref[...]
        y = x * lax.rsqrt(jnp.mean(x * x, axis=-1, keepdims=True) + EPS)
        xn_ref[...] = (y * g_ref[...]).astype(BF16)

    o_ref[...] = jnp.dot(xn_ref[...], w_ref[...], preferred_element_type=F32)


def _inproj(h, g, w_bf16, tn=1024):
    n_rows, d = h.shape
    n_out = w_bf16.shape[1]
    return pl.pallas_call(
        _inproj_body,
        grid=(n_rows // ROW_TILE, n_out // tn),
        in_specs=[
            pl.BlockSpec((ROW_TILE, d), lambda i, j: (i, 0)),
            pl.BlockSpec((1, d), lambda i, j: (0, 0)),
            pl.BlockSpec((d, tn), lambda i, j: (0, j)),
        ],
        out_specs=pl.BlockSpec((ROW_TILE, tn), lambda i, j: (i, j)),
        out_shape=jax.ShapeDtypeStruct((n_rows, n_out), F32),
        scratch_shapes=[pltpu.VMEM((ROW_TILE, d), BF16)],
        compiler_params=_cparams(("parallel", "arbitrary")),
        name="inproj",
    )(h, g.reshape(1, d), w_bf16)


def _shortconv_body(prev_ref, cur_ref, next_ref, w_ref, b_ref, o_ref):
    i = pl.program_id(0)
    last = pl.num_programs(0) - 1
    x = cur_ref[...]
    rows = lax.broadcasted_iota(jnp.int32, x.shape, 0)
    prev_row = jnp.where(i == 0, 0.0, prev_ref[7:8, :])
    next_row = jnp.where(i == last, 0.0, next_ref[0:1, :])
    up = jnp.where(rows == 0, prev_row, pltpu.roll(x, 1, 0))
    dn = jnp.where(rows == ROW_TILE - 1, next_row, pltpu.roll(x, ROW_TILE - 1, 0))
    y = up * w_ref[0:1, :]
    y = y + x * w_ref[1:2, :]
    y = y + dn * w_ref[2:3, :]
    o_ref[...] = y + b_ref[...]


def _shortconv(z, w, b, tc=1024):
    n_rows = z.shape[0]
    n_c = w.shape[1]
    n8 = n_rows // 8
    r8 = ROW_TILE // 8
    return pl.pallas_call(
        _shortconv_body,
        grid=(n_rows // ROW_TILE, n_c // tc),
        in_specs=[
            pl.BlockSpec((8, tc), lambda i, j: (jnp.maximum(i * r8 - 1, 0), j)),
            pl.BlockSpec((ROW_TILE, tc), lambda i, j: (i, j)),
            pl.BlockSpec((8, tc), lambda i, j: (jnp.minimum((i + 1) * r8, n8 - 1), j)),
            pl.BlockSpec((3, tc), lambda i, j: (0, j)),
            pl.BlockSpec((1, tc), lambda i, j: (0, j)),
        ],
        out_specs=pl.BlockSpec((ROW_TILE, tc), lambda i, j: (i, j)),
        out_shape=jax.ShapeDtypeStruct((n_rows, n_c), F32),
        compiler_params=_cparams(("parallel", "parallel")),
        name="shortconv",
    )(z, z, z, w, b.reshape(1, n_c))


def _hp_dot(a, b):
    return jnp.dot(a, b, preferred_element_type=F32, precision=lax.Precision.HIGHEST)


def _filter_body(z_ref, w1_ref, b1_ref, w2_ref, b2_ref, w3_ref, b3_ref, fr_ref, w4_ref, dec_ref, o_ref, h_ref):
    i = pl.program_id(0)
    j = pl.program_id(1)

    @pl.when(j == 0)
    def _():
        fr = fr_ref[...]
        h = jnp.sin(fr * (_hp_dot(z_ref[...], w1_ref[...]) + b1_ref[...]))
        h = jnp.sin(fr * (_hp_dot(h, w2_ref[...]) + b2_ref[...]))
        h_ref[...] = jnp.sin(fr * (_hp_dot(h, w3_ref[...]) + b3_ref[...]))

    t = z_ref[:, 0:1]
    y = _hp_dot(h_ref[...], w4_ref[...]) * jnp.exp(-t * jnp.abs(dec_ref[...]))
    rows = lax.broadcasted_iota(jnp.int32, y.shape, 0)
    drop = jnp.logical_and(jnp.logical_and(i == 0, j % 2 == 1), rows == 0)
    o_ref[...] = jnp.where(drop, 0.0, y)


def _hyena_filters(w1, b1, w2, b2, w3, b3, w4, freq, decay):
    zfeat = jnp.asarray(_filter_features(), dtype=F32)
    w1p = jnp.zeros((HY_HID, HY_HID), F32).at[:HY_EMB].set(w1)
    n_c = w4.shape[1]
    tc = C_BR
    small = lambda i, j: (0, 0)
    return pl.pallas_call(
        _filter_body,
        grid=(SEQ_T // ROW_TILE, n_c // tc),
        in_specs=[
            pl.BlockSpec((ROW_TILE, HY_HID), lambda i, j: (i, 0)),
            pl.BlockSpec((HY_HID, HY_HID), small),
            pl.BlockSpec((1, HY_HID), small),
            pl.BlockSpec((HY_HID, HY_HID), small),
            pl.BlockSpec((1, HY_HID), small),
            pl.BlockSpec((HY_HID, HY_HID), small),
            pl.BlockSpec((1, HY_HID), small),
            pl.BlockSpec((1, HY_HID), small),
            pl.BlockSpec((HY_HID, tc), lambda i, j: (0, j)),
            pl.BlockSpec((1, tc), lambda i, j: (0, j)),
        ],
        out_specs=pl.BlockSpec((ROW_TILE, tc), lambda i, j: (i, j)),
        out_shape=jax.ShapeDtypeStruct((SEQ_T, n_c), F32),
        scratch_shapes=[pltpu.VMEM((ROW_TILE, HY_HID), F32)],
        compiler_params=_cparams(("parallel", "arbitrary")),
        name="hyena_filter",
    )(zfeat, w1p, b1.reshape(1, -1), w2, b2.reshape(1, -1), w3, b3.reshape(1, -1), freq.reshape(1, -1),
      w4, decay.reshape(1, n_c))


def _stage_a_body(n_in, n_out, *refs):
    x_refs = refs[:n_in]
    w_refs = refs[n_in:n_in + n_in * n_out]
    o_refs = refs[n_in + n_in * n_out:]
    bb = pl.program_id(0)
    for j in range(BBLK):
        valid = bb * BBLK + j < FB
        xs = [jnp.where(valid, xr[:, j, :], 0.0).astype(BF16) for xr in x_refs]
        for o in range(n_out):
            acc = None
            for k in range(n_in):
                d = jnp.dot(w_refs[o * n_in + k][j], xs[k], preferred_element_type=F32)
                acc = d if acc is None else acc + d
            o_refs[o][j] = acc.astype(o_refs[o].dtype)


def _stage_a(xs, col_offs, n_cols, w_tabs, out_dtype, ct=1024):
    n_in = len(xs)
    n_out = len(w_tabs)
    m = w_tabs[0][0].shape[1]
    x3 = [x.reshape(FA, FB, x.shape[1]) for x in xs]
    in_specs = []
    for k in range(n_in):
        off = col_offs[k] // ct
        in_specs.append(pl.BlockSpec((FA, BBLK, ct), lambda b, c, off=off: (0, b, off + c)))
    flat_w = []
    for o in range(n_out):
        for k in range(n_in):
            flat_w.append(w_tabs[o][k])
            in_specs.append(pl.BlockSpec((BBLK, m, FA), lambda b, c: (b, 0, 0)))
    return pl.pallas_call(
        functools.partial(_stage_a_body, n_in, n_out),
        grid=(FBP // BBLK, n_cols // ct),
        in_specs=in_specs,
        out_specs=[pl.BlockSpec((BBLK, m, ct), lambda b, c: (b, 0, c)) for _ in range(n_out)],
        out_shape=[jax.ShapeDtypeStruct((FBP, m, n_cols), out_dtype) for _ in range(n_out)],
        compiler_params=_cparams(("parallel", "parallel")),
        name="dft_stage_a",
    )(*x3, *flat_w)


def _stage_ainv_body(tr_ref, ti_ref, wr_ref, wi_ref, v_ref, x_ref, skip_ref, o_ref):
    skip = skip_ref[...]
    for j in range(BBLK):
        y = jnp.dot(wr_ref[j], tr_ref[j].astype(BF16), preferred_element_type=F32)
        y = y + jnp.dot(wi_ref[j], ti_ref[j].astype(BF16), preferred_element_type=F32)
        y = y + v_ref[:, j, :] * skip
        o_ref[:, j, :] = x_ref[:, j, :] * y


def _stage_ainv(tr, ti, wr, wi, v, v_off, xmul, x_off, skip, ct=1024):
    n_cols = tr.shape[2]
    kap = tr.shape[1]
    view = lambda x: x.reshape(FA, FB, x.shape[1])
    spec3 = lambda off: pl.BlockSpec((FA, BBLK, ct), lambda b, c, off=off // ct: (0, b, off + c))
    in_specs = [
        pl.BlockSpec((BBLK, kap, ct), lambda b, c: (b, 0, c)),
        pl.BlockSpec((BBLK, kap, ct), lambda b, c: (b, 0, c)),
        pl.BlockSpec((BBLK, FA, kap), lambda b, c: (b, 0, 0)),
        pl.BlockSpec((BBLK, FA, kap), lambda b, c: (b, 0, 0)),
        spec3(v_off),
        spec3(x_off),
        pl.BlockSpec((1, ct), lambda b, c: (0, c)),
    ]
    args = [tr, ti, wr, wi, view(v), view(xmul), skip.reshape(1, n_cols)]
    out = pl.pallas_call(
        _stage_ainv_body,
        grid=(FBP // BBLK, n_cols // ct),
        in_specs=in_specs,
        out_specs=pl.BlockSpec((FA, BBLK, ct), lambda b, c: (0, b, c)),
        out_shape=jax.ShapeDtypeStruct((FA, FB, n_cols), F32),
        compiler_params=_cparams(("parallel", "parallel")),
        name="dft_stage_a_inv",
    )(*args)
    return out.reshape(SEQ_T, n_cols)


KBLK = 8


def _filter_spec_body(fr_ref, fi_ref, br_ref, bi_ref, c_ref, s_ref, kr_ref, ki_ref):
    c = c_ref[...]
    s = s_ref[...]
    dot = lambda a, b: jnp.dot(a, b, preferred_element_type=F32)
    for j in range(KBLK):
        fr, fi = fr_ref[:, j, :].astype(BF16), fi_ref[:, j, :].astype(BF16)
        br, bi = br_ref[:, j, :].astype(BF16), bi_ref[:, j, :].astype(BF16)
        kr_ref[j] = dot(c, fr) + dot(s, fi) + dot(c, br) + dot(s, bi)
        ki_ref[j] = dot(c, fi) - dot(s, fr) - dot(c, bi) + dot(s, br)


def _filter_spectrum(tr, ti, cb, sb, ct=256):
    nc = C_BR // ct
    kap = tr.shape[1]

    def tspec(direction):
        return pl.BlockSpec((FBP, KBLK, ct), lambda o, k, c: (0, k, (2 * o + direction) * nc + c))

    mat = pl.BlockSpec((FBP, FBP), lambda o, k, c: (0, 0))
    ospec = pl.BlockSpec((None, KBLK, FBP, ct), lambda o, k, c: (o, k, 0, c))
    oshape = jax.ShapeDtypeStruct((2, kap, FBP, C_BR), F32)
    return pl.pallas_call(
        _filter_spec_body,
        grid=(2, kap // KBLK, nc),
        in_specs=[tspec(0), tspec(0), tspec(1), tspec(1), mat, mat],
        out_specs=[ospec, ospec],
        out_shape=[oshape, oshape],
        compiler_params=_cparams(("parallel", "parallel", "parallel")),
        name="filter_spectrum",
    )(tr, ti, tr, ti, cb, sb)


def _conv_b_body(tr_ref, ti_ref, kr_ref, ki_ref, c_ref, s_ref, or_ref, oi_ref):
    c = c_ref[...]
    s = s_ref[...]
    dot = lambda a, b: jnp.dot(a, b, preferred_element_type=F32)
    for j in range(KBLK):
        tr = tr_ref[:, j, :].astype(BF16)
        ti = ti_ref[:, j, :].astype(BF16)
        yr = dot(c, tr) + dot(s, ti)
        yi = dot(c, ti) - dot(s, tr)
        kr = kr_ref[j]
        ki = ki_ref[j]
        zr = (yr * kr - yi * ki).astype(BF16)
        zi = (yr * ki + yi * kr).astype(BF16)
        or_ref[:, j, :] = dot(c, zr) - dot(s, zi)
        oi_ref[:, j, :] = dot(c, zi) + dot(s, zr)


def _conv_stage_b(tr, ti, kr, ki, order, cb, sb, ct=256):
    kap = tr.shape[1]
    n_cols = tr.shape[2]
    tspec = pl.BlockSpec((FBP, KBLK, ct), lambda k, c: (0, k, c))
    kspec = pl.BlockSpec((None, KBLK, FBP, ct), lambda k, c: (order, k, 0, c))
    mat = pl.BlockSpec((FBP, FBP), lambda k, c: (0, 0))
    oshape = jax.ShapeDtypeStruct((FBP, kap, n_cols), F32)
    return pl.pallas_call(
        _conv_b_body,
        grid=(kap // KBLK, n_cols // ct),
        in_specs=[tspec, tspec, kspec, kspec, mat, mat],
        out_specs=[tspec, tspec],
        out_shape=[oshape, oshape],
        compiler_params=_cparams(("parallel", "parallel")),
        name="conv_stage_b",
    )(tr, ti, kr, ki, cb, sb)


def _fnet_b_body(tr_ref, ti_ref, c_ref, s_ref, o_ref):
    dot = lambda a, b: jnp.dot(a, b, preferred_element_type=F32)
    y = dot(c_ref[...], tr_ref[...]) + dot(s_ref[...], ti_ref[...])
    o_ref[...] = y[:FB, :]


def _fnet_stage_b(tr, ti, cb, sb, ct=1024):
    n_cols = tr.shape[2]
    nc = n_cols // ct
    tr2 = tr.reshape(FBP, FA * n_cols)
    ti2 = ti.reshape(FBP, FA * n_cols)
    tspec = pl.BlockSpec((FBP, ct), lambda ka, c: (0, ka * nc + c))
    mat = pl.BlockSpec((FBP, FBP), lambda ka, c: (0, 0))
    out = pl.pallas_call(
        _fnet_b_body,
        grid=(FA, nc),
        in_specs=[tspec, tspec, mat, mat],
        out_specs=pl.BlockSpec((FB, ct), lambda ka, c: (0, ka * nc + c)),
        out_shape=jax.ShapeDtypeStruct((FB, FA * n_cols), F32),
        compiler_params=_cparams(("parallel", "parallel")),
        name="fnet_stage_b",
    )(tr2, ti2, cb, sb)
    return out.reshape(SEQ_T, n_cols)


def _chan_dft_body(x_ref, w_ref, p_ref, q_ref):
    y = jnp.dot(x_ref[...].astype(BF16), w_ref[...], preferred_element_type=F32)
    p_ref[...] = y[:, :256]
    q_ref[...] = y[:, 256:]


def _chan_dft(z, off, chan):
    oshape = jax.ShapeDtypeStruct((SEQ_T, C_BR), F32)
    return pl.pallas_call(
        _chan_dft_body,
        grid=(SEQ_T // ROW_TILE, 4),
        in_specs=[
            pl.BlockSpec((ROW_TILE, 256), lambda i, g: (i, off // 256 + g)),
            pl.BlockSpec((256, 512), lambda i, g: (0, 0)),
        ],
        out_specs=[pl.BlockSpec((ROW_TILE, 256), lambda i, g: (i, g))] * 2,
        out_shape=[oshape, oshape],
        compiler_params=_cparams(("parallel", "parallel")),
        name="fnet_chan_dft",
    )(z, chan)


NA_QROWS = 8
NA_SLAB = 16
NA_SCALE = NA_DH ** -0.5


def _na_main_body(q_ref, k_ref, v_ref, km_ref, vm_ref, bt_ref, mb_ref, g_ref, o_ref):
    rb = pl.program_id(1)
    slab0 = jnp.clip(rb * NA_QROWS - NA_KH // 2, 0, GRID_H - NA_SLAB)
    w2 = 2 * NA_DH
    lane = lax.broadcasted_iota(jnp.int32, (GRID_W, w2), 1)
    row2 = lax.broadcasted_iota(jnp.int32, (w2, w2), 0)
    lane2 = lax.broadcasted_iota(jnp.int32, (w2, w2), 1)
    own_head = (row2 >= NA_DH) == (lane2 >= NA_DH)
    km = km_ref[...].astype(BF16)
    vm = vm_ref[...].astype(BF16)
    mbt = mb_ref[...]
    dn_t = (((1,), (1,)), ((), ()))
    dn_k = (((0,), (0,)), ((), ()))
    for i in range(NA_QROWS):
        r = rb * NA_QROWS + i
        r0 = jnp.clip(r - NA_KH // 2, 0, GRID_H - NA_KH)
        off = pl.multiple_of((r0 - slab0) * GRID_W, GRID_W)
        d0 = r0 - r + (NA_KH - 1)
        q = q_ref[i * GRID_W:(i + 1) * GRID_W, :]
        q2 = jnp.where(own_head, jnp.concatenate([q, q], axis=0), 0.0).astype(BF16)
        ks = k_ref[pl.ds(off, NA_KH * GRID_W), :].astype(BF16)
        vs = v_ref[pl.ds(off, NA_KH * GRID_W), :].astype(BF16)
        st = lax.dot_general(ks, q2, dn_t, preferred_element_type=F32) * NA_SCALE + bt_ref[d0]
        sx = lax.dot_general(km, q2, dn_t, preferred_element_type=F32) * NA_SCALE + mbt
        m = jnp.maximum(jnp.max(st, axis=0, keepdims=True), jnp.max(sx, axis=0, keepdims=True))
        p = jnp.exp(st - m)
        px = jnp.exp(sx - m)
        inv = 1.0 / (jnp.sum(p, axis=0, keepdims=True) + jnp.sum(px, axis=0, keepdims=True))
        o2 = lax.dot_general((p * inv).astype(BF16), vs, dn_k, preferred_element_type=F32)
        o2 = o2 + lax.dot_general((px * inv).astype(BF16), vm, dn_k, preferred_element_type=F32)
        o = jnp.where(lane < NA_DH, o2[:GRID_W], o2[GRID_W:])
        g = g_ref[i * GRID_W:(i + 1) * GRID_W, :]
        o_ref[i * GRID_W:(i + 1) * GRID_W, :] = (o * (g * jax.nn.sigmoid(g))).astype(BF16)


def _na_meta_body(q_ref, k_ref, v_ref, mb_ref, g_ref, o_ref):
    lane = lax.broadcasted_iota(jnp.int32, (N_META, 2 * NA_DH), 1)
    q = q_ref[...]
    km = k_ref[...].astype(BF16)
    vm = v_ref[...].astype(BF16)
    dn_t = (((1,), (1,)), ((), ()))
    outs = []
    for hh in range(2):
        sel = (lane >= hh * NA_DH) & (lane < (hh + 1) * NA_DH)
        qm = jnp.where(sel, q, 0.0).astype(BF16)
        s = lax.dot_general(qm, km, dn_t, preferred_element_type=F32) * NA_SCALE + mb_ref[hh, 0:1, :]
        m = jnp.max(s, axis=-1, keepdims=True)
        p = jnp.exp(s - m)
        den = jnp.sum(p, axis=-1, keepdims=True)
        outs.append(jnp.dot(p.astype(BF16), vm, preferred_element_type=F32) / den)
    o = jnp.where(lane < NA_DH, outs[0], outs[1])
    g = g_ref[...]
    o_ref[...] = (o * (g * jax.nn.sigmoid(g))).astype(BF16)


def _na_bias_table(rpb):
    colmask, dc = _na_index_tables()
    rows = np.arange(NA_KH)[:, None] + np.arange(NA_KH)[None, :]
    t = rpb.astype(F32)[:, rows]
    t = t[:, :, :, dc]
    t = jnp.where(colmask[None, None, None], t, NEG_INF)
    t = t.reshape(NA_HEADS // 2, 2, NA_KH, NA_KH, GRID_W, GRID_W)
    t = jnp.transpose(t, (0, 2, 3, 5, 1, 4))
    return t.reshape(NA_HEADS // 2, NA_KH, NA_KH * GRID_W, 2 * GRID_W)


def _neighborhood_attention(z, rpb, meta_bias):
    bt = _na_bias_table(rpb)
    mb = jnp.broadcast_to(meta_bias.astype(F32).reshape(NA_HEADS // 2, 2, 1, N_META),
                          (NA_HEADS // 2, 2, 8, N_META))
    mbt = jnp.transpose(meta_bias.astype(F32).reshape(NA_HEADS // 2, 2, N_META), (0, 2, 1))
    mbt = jnp.repeat(mbt, GRID_W, axis=2)
    w2 = 2 * NA_DH
    qrows = NA_QROWS * GRID_W
    srows = NA_SLAB * GRID_W

    def slab_start(rb):
        start = N_META + GRID_W * jnp.clip(rb * NA_QROWS - NA_KH // 2, 0, GRID_H - NA_SLAB)
        return pl.multiple_of(start, N_META)

    def col(off):
        return lambda hp, rb: (pl.multiple_of(N_META + rb * qrows, N_META), pl.multiple_of(off + hp * w2, w2))

    def slab(off):
        return lambda hp, rb: (slab_start(rb), pl.multiple_of(off + hp * w2, w2))

    def meta(off):
        return lambda hp, rb: (0, off // w2 + hp)

    y_main = pl.pallas_call(
        _na_main_body,
        grid=(NA_HEADS // 2, GRID_H // NA_QROWS),
        in_specs=[
            pl.BlockSpec((pl.Element(qrows), pl.Element(w2)), col(OFF_Q)),
            pl.BlockSpec((pl.Element(srows), pl.Element(w2)), slab(OFF_K)),
            pl.BlockSpec((pl.Element(srows), pl.Element(w2)), slab(OFF_V)),
            pl.BlockSpec((N_META, w2), meta(OFF_K)),
            pl.BlockSpec((N_META, w2), meta(OFF_V)),
            pl.BlockSpec((None, NA_KH, NA_KH * GRID_W, w2), lambda hp, rb: (hp, 0, 0, 0)),
            pl.BlockSpec((None, N_META, w2), lambda hp, rb: (hp, 0, 0)),
            pl.BlockSpec((pl.Element(qrows), pl.Element(w2)), col(OFF_NA_GATE)),
        ],
        out_specs=pl.BlockSpec((qrows, w2), lambda hp, rb: (rb, hp)),
        out_shape=jax.ShapeDtypeStruct((SEQ, C_BR), BF16),
        compiler_params=_cparams(("parallel", "parallel")),
        name="na_main",
    )(z, z, z, z, z, bt, mbt, z)

    def mcol(off):
        return lambda hp: (0, off // w2 + hp)

    y_meta = pl.pallas_call(
        _na_meta_body,
        grid=(NA_HEADS // 2,),
        in_specs=[
            pl.BlockSpec((N_META, w2), mcol(OFF_Q)),
            pl.BlockSpec((N_META, w2), mcol(OFF_K)),
            pl.BlockSpec((N_META, w2), mcol(OFF_V)),
            pl.BlockSpec((None, 2, 8, N_META), lambda hp: (hp, 0, 0, 0)),
            pl.BlockSpec((N_META, w2), mcol(OFF_NA_GATE)),
        ],
        out_specs=pl.BlockSpec((N_META, w2), lambda hp: (0, hp)),
        out_shape=jax.ShapeDtypeStruct((N_META, C_BR), BF16),
        compiler_params=_cparams(("parallel",)),
        name="na_meta",
    )(z, z, z, mb, z)
    return y_meta, y_main


def _silu(g):
    return g * jax.nn.sigmoid(g)


def _merge_body(ya_ref, yb_ref, yc_ref, hg_ref, fg_ref, wa_ref, wb_ref, wc_ref, ga_ref, gb_ref, gc_ref, o_ref):
    dot = lambda a, b: jnp.dot(a.astype(BF16), b[...], preferred_element_type=F32)
    m = jax.nn.sigmoid(ga_ref[...]) * dot(ya_ref[...] * _silu(hg_ref[...]), wa_ref)
    m = m + jax.nn.sigmoid(gb_ref[...]) * dot(yb_ref[...] * _silu(fg_ref[...]), wb_ref)
    m = m + jax.nn.sigmoid(gc_ref[...]) * dot(yc_ref[...], wc_ref)
    o_ref[...] = m.astype(BF16)


def _merge(ya, yb, yc, wa, wb, wc, z, tn=512):
    n_rows = ya.shape[0]
    yspec = pl.BlockSpec((ROW_TILE, C_BR), lambda i, j: (i, 0))
    wspec = pl.BlockSpec((C_BR, tn), lambda i, j: (0, j))
    zspec = lambda off: pl.BlockSpec((ROW_TILE, C_BR), lambda i, j, off=off: (i, off // C_BR))
    gspec = lambda k: pl.BlockSpec((ROW_TILE, tn), lambda i, j, k=k: (i, (OFF_MERGE + k * D_MODEL) // tn + j))
    return pl.pallas_call(
        _merge_body,
        grid=(n_rows // ROW_TILE, D_MODEL // tn),
        in_specs=[yspec, yspec, yspec, zspec(OFF_HY_GATE), zspec(OFF_FN_GATE), wspec, wspec, wspec,
                  gspec(0), gspec(1), gspec(2)],
        out_specs=pl.BlockSpec((ROW_TILE, tn), lambda i, j: (i, j)),
        out_shape=jax.ShapeDtypeStruct((n_rows, D_MODEL), BF16),
        compiler_params=_cparams(("parallel", "parallel")),
        name="merge",
    )(ya, yb, yc, z, z, wa, wb, wc, z, z, z)


def _outproj_body(h_ref, m_ref, w_ref, o_ref):
    o_ref[...] = h_ref[...] + jnp.dot(m_ref[...], w_ref[...], preferred_element_type=F32)


def _outproj(h, m, w, tn=1024):
    n_rows = h.shape[0]
    return pl.pallas_call(
        _outproj_body,
        grid=(n_rows // ROW_TILE, D_MODEL // tn),
        in_specs=[
            pl.BlockSpec((ROW_TILE, tn), lambda i, j: (i, j)),
            pl.BlockSpec((ROW_TILE, D_MODEL), lambda i, j: (i, 0)),
            pl.BlockSpec((D_MODEL, tn), lambda i, j: (0, j)),
        ],
        out_specs=pl.BlockSpec((ROW_TILE, tn), lambda i, j: (i, j)),
        out_shape=jax.ShapeDtypeStruct((n_rows, D_MODEL), F32),
        compiler_params=_cparams(("parallel", "parallel")),
        name="outproj",
    )(h, m, w)


def _final_norm_body(x_ref, g_ref, o_ref):
    x = x_ref[...]
    y = x * lax.rsqrt(jnp.mean(x * x, axis=-1, keepdims=True) + EPS)
    o_ref[...] = y * g_ref[...]


def _final_norm(h, g, tm=512):
    return pl.pallas_call(
        _final_norm_body,
        grid=(SEQ // tm,),
        in_specs=[
            pl.BlockSpec((pl.Element(tm), pl.Element(D_MODEL)), lambda i: (pl.multiple_of(N_META + i * tm, N_META), 0)),
            pl.BlockSpec((1, D_MODEL), lambda i: (0, 0)),
        ],
        out_specs=pl.BlockSpec((tm, D_MODEL), lambda i: (i, 0)),
        out_shape=jax.ShapeDtypeStruct((SEQ, D_MODEL), F32),
        compiler_params=_cparams(("parallel",)),
        name="final_norm",
    )(h, g.reshape(1, D_MODEL))


def _hyena_branch(z, conv_w, conv_b, w1, b1, w2, b2, w3, b3, w4, freq, decay, skip):
    tab = _hyena_tables()
    fwd = [[_bf(tab["fwd_r"])], [_bf(tab["fwd_i"])]]
    inv_r, inv_i = _bf(tab["inv_r"]), _bf(tab["inv_i"])
    cb, sb = _bf(tab["cb"]), _bf(tab["sb"])

    hyc = _shortconv(z, conv_w, conv_b)
    filt = _hyena_filters(w1, b1, w2, b2, w3, b3, w4, freq, decay)
    ftr, fti = _stage_a([filt], [0], filt.shape[1], fwd, F32)
    kr, ki = _filter_spectrum(ftr, fti, cb, sb)

    tr, ti = _stage_a([hyc], [0], C_BR, fwd, F32)
    tr, ti = _conv_stage_b(tr, ti, kr, ki, 0, cb, sb)
    zmid = _stage_ainv(tr, ti, inv_r, inv_i, hyc, 0, hyc, C_BR, skip[0])

    tr, ti = _stage_a([zmid], [0], C_BR, fwd, F32)
    tr, ti = _conv_stage_b(tr, ti, kr, ki, 1, cb, sb)
    return _stage_ainv(tr, ti, inv_r, inv_i, zmid, 0, hyc, 2 * C_BR, skip[1])


def _fnet_branch(z):
    tab = _fnet_tables()
    cs, sn = _bf(tab["cs"]), _bf(tab["sn"])
    p, q = _chan_dft(z, OFF_FN_IN, _bf(tab["chan"]))
    tr, ti = _stage_a([p, q], [0, 0], C_BR, [[cs, -sn], [-sn, -cs]], BF16)
    return _fnet_stage_b(tr, ti, _bf(tab["cb"]), _bf(tab["sb"]))


def _layer(h, norm_g, w_in, conv_w, conv_b, w1, b1, w2, b2, w3, b3, w4, freq, decay, skip, rpb, meta_bias,
           w_a, w_b, w_c, w_out):
    z = _inproj(h, norm_g, w_in.astype(BF16))
    ya = _hyena_branch(z, conv_w, conv_b, w1, b1, w2, b2, w3, b3, w4, freq, decay, skip)
    yb = _fnet_branch(z)
    yc_meta, yc_main = _neighborhood_attention(z, rpb, meta_bias)
    yc = jnp.concatenate([yc_meta, yc_main], axis=0)
    m = _merge(ya, yb, yc, w_a.astype(BF16), w_b.astype(BF16), w_c.astype(BF16), z)
    return _outproj(h, m, w_out.astype(BF16))


def kernel(x, meta_tokens, norm_g, w_in, hy_conv_w, hy_conv_b, hy_flt_w1, hy_flt_b1, hy_flt_w2, hy_flt_b2,
           hy_flt_w3, hy_flt_b3, hy_flt_w4, hy_flt_freq, hy_decay, hy_skip, na_rpb, na_meta_bias,
           w_branch_a, w_branch_b, w_branch_c, w_out, final_g):
    assert x.shape == (1, SEQ, D_MODEL)
    h = jnp.concatenate([meta_tokens.astype(x.dtype), x[0]], axis=0)
    for i in range(norm_g.shape[0]):
        h = _layer(h, norm_g[i], w_in[i], hy_conv_w[i], hy_conv_b[i], hy_flt_w1[i], hy_flt_b1[i],
                   hy_flt_w2[i], hy_flt_b2[i], hy_flt_w3[i], hy_flt_b3[i], hy_flt_w4[i], hy_flt_freq[i],
                   hy_decay[i].reshape(-1), hy_skip[i], na_rpb[i], na_meta_bias[i],
                   w_branch_a[i], w_branch_b[i], w_branch_c[i], w_out[i])
    return _final_norm(h, final_g)[None]
```

```python
import functools
import math

import numpy as np
import jax
import jax.numpy as jnp
from jax import lax
from jax.experimental import pallas as pl
from jax.experimental.pallas import tpu as pltpu

F32 = jnp.float32
BF16 = jnp.bfloat16

D_MODEL = 2048
SEQ = 16384
N_META = 16
SEQ_T = SEQ + N_META
GRID_W = 64
GRID_H = SEQ // GRID_W
C_BR = 1024
HY_EMB = 33
HY_HID = 64
NA_HEADS = 16
NA_DH = 64
NA_KH = 8
NA_KW = 16
EPS = 1e-6
NEG_INF = -1e30

OFF_HY_IN = 0
OFF_HY_GATE = 3072
OFF_FN_IN = 4096
OFF_FN_GATE = 5120
OFF_Q = 6144
OFF_K = 7168
OFF_V = 8192
OFF_NA_GATE = 9216
OFF_MERGE = 10240
N_IN = 16384

FA = 80
FB = 205
FBP = 208
BBLK = 8
KA_HY = FA + 1
KAP_HY = 96

ROW_TILE = 656
LANE = 128
VMEM_LIMIT = 48 * 1024 * 1024


def _cparams(sem):
    return pltpu.CompilerParams(dimension_semantics=sem, vmem_limit_bytes=VMEM_LIMIT)


@functools.lru_cache(maxsize=None)
def _hyena_tables():
    n_circ = 2 * SEQ_T
    a = np.arange(FA)[None, None, :]
    b = np.arange(FBP)[:, None, None]
    ka = np.arange(KAP_HY)[None, :, None]
    n = FB * a + b
    ang = 2.0 * np.pi * ((ka * n) % n_circ) / n_circ
    valid = (b < FB) & (ka < KA_HY)
    fwd_r = np.where(valid, np.cos(ang), 0.0)
    fwd_i = np.where(valid, -np.sin(ang), 0.0)
    c = np.where((ka == 0) | (ka == FA), 1.0, 2.0) / n_circ
    inv_r = np.transpose(np.where(valid, c * np.cos(ang), 0.0), (0, 2, 1))
    inv_i = np.transpose(np.where(valid, -c * np.sin(ang), 0.0), (0, 2, 1))
    kb = np.arange(FBP)[:, None]
    bb = np.arange(FBP)[None, :]
    phi = 2.0 * np.pi * ((kb * bb) % FB) / FB
    ok = (kb < FB) & (bb < FB)
    cb = np.where(ok, np.cos(phi), 0.0)
    sb = np.where(ok, np.sin(phi), 0.0)
    return dict(fwd_r=fwd_r, fwd_i=fwd_i, inv_r=inv_r, inv_i=inv_i, cb=cb, sb=sb)


@functools.lru_cache(maxsize=None)
def _fnet_tables():
    a = np.arange(FA)[None, None, :]
    b = np.arange(FBP)[:, None, None]
    ka = np.arange(FA)[None, :, None]
    n = FB * a + b
    ang = 2.0 * np.pi * ((ka * n) % SEQ_T) / SEQ_T
    valid = np.broadcast_to(b < FB, ang.shape)
    cs = np.where(valid, np.cos(ang), 0.0)
    sn = np.where(valid, np.sin(ang), 0.0)
    kb = np.arange(FBP)[:, None]
    bb = np.arange(FBP)[None, :]
    phi = 2.0 * np.pi * ((kb * bb) % FB) / FB
    ok = (kb < FB) & (bb < FB)
    scale = 1.0 / math.sqrt(SEQ_T * 256.0)
    cb = np.where(ok, np.cos(phi), 0.0) * scale
    sb = np.where(ok, np.sin(phi), 0.0) * scale
    j = np.arange(256)
    th = 2.0 * np.pi * ((j[:, None] * j[None, :]) % 256) / 256.0
    chan = np.concatenate([np.cos(th), np.sin(th)], axis=1)
    return dict(cs=cs, sn=sn, cb=cb, sb=sb, chan=chan)


@functools.lru_cache(maxsize=None)
def _filter_features():
    t = np.linspace(0.0, 1.0, SEQ_T)[:, None]
    bands = (HY_EMB - 1) // 2
    w = 2.0 * np.pi * np.arange(SEQ_T)[:, None] / SEQ_T
    f = np.linspace(1e-4, bands - 1, bands)[None, :]
    z = np.concatenate([t, np.cos(f * w), -np.sin(f * w)], axis=-1)
    out = np.zeros((SEQ_T, HY_HID), np.float64)
    out[:, :HY_EMB] = z
    return out


@functools.lru_cache(maxsize=None)
def _na_index_tables():
    qc = np.arange(GRID_W)[:, None]
    kc = np.arange(GRID_W)[None, :]
    cs = np.clip(qc - NA_KW // 2, 0, GRID_W - NA_KW)
    colmask = (kc >= cs) & (kc < cs + NA_KW)
    dc = np.clip(kc - qc + NA_KW - 1, 0, 2 * NA_KW - 2)
    return colmask, dc


def _bf(x):
    return jnp.asarray(np.asarray(x, np.float32), dtype=BF16)


def _inproj_body(x_ref, g_ref, w_ref, o_ref, xn_ref):
    @pl.when(pl.program_id(1) == 0)
    def _():
        x = x_ref[...]
        y = x * lax.rsqrt(jnp.mean(x * x, axis=-1, keepdims=True) + EPS)
        xn_ref[...] = (y * g_ref[...]).astype(BF16)

    o_ref[...] = jnp.dot(xn_ref[...], w_ref[...], preferred_element_type=F32)


def _inproj(h, g, w_bf16, tn=1024):
    n_rows, d = h.shape
    n_out = w_bf16.shape[1]
    return pl.pallas_call(
        _inproj_body,
        grid=(n_rows // ROW_TILE, n_out // tn),
        in_specs=[
            pl.BlockSpec((ROW_TILE, d), lambda i, j: (i, 0)),
            pl.BlockSpec((1, d), lambda i, j: (0, 0)),
            pl.BlockSpec((d, tn), lambda i, j: (0, j)),
        ],
        out_specs=pl.BlockSpec((ROW_TILE, tn), lambda i, j: (i, j)),
        out_shape=jax.ShapeDtypeStruct((n_rows, n_out), F32),
        scratch_shapes=[pltpu.VMEM((ROW_TILE, d), BF16)],
        compiler_params=_cparams(("parallel", "arbitrary")),
        name="inproj",
    )(h, g.reshape(1, d), w_bf16)


def _shortconv_body(prev_ref, cur_ref, next_ref, w_ref, b_ref, o_ref):
    i = pl.program_id(0)
    last = pl.num_programs(0) - 1
    x = cur_ref[...]
    rows = lax.broadcasted_iota(jnp.int32, x.shape, 0)
    prev_row = jnp.where(i == 0, 0.0, prev_ref[7:8, :])
    next_row = jnp.where(i == last, 0.0, next_ref[0:1, :])
    up = jnp.where(rows == 0, prev_row, pltpu.roll(x, 1, 0))
    dn = jnp.where(rows == ROW_TILE - 1, next_row, pltpu.roll(x, ROW_TILE - 1, 0))
    y = up * w_ref[0:1, :]
    y = y + x * w_ref[1:2, :]
    y = y + dn * w_ref[2:3, :]
    o_ref[...] = y + b_ref[...]


def _shortconv(z, w, b, tc=1024):
    n_rows = z.shape[0]
    n_c = w.shape[1]
    n8 = n_rows // 8
    r8 = ROW_TILE // 8
    return pl.pallas_call(
        _shortconv_body,
        grid=(n_rows // ROW_TILE, n_c // tc),
        in_specs=[
            pl.BlockSpec((8, tc), lambda i, j: (jnp.maximum(i * r8 - 1, 0), j)),
            pl.BlockSpec((ROW_TILE, tc), lambda i, j: (i, j)),
            pl.BlockSpec((8, tc), lambda i, j: (jnp.minimum((i + 1) * r8, n8 - 1), j)),
            pl.BlockSpec((3, tc), lambda i, j: (0, j)),
            pl.BlockSpec((1, tc), lambda i, j: (0, j)),
        ],
        out_specs=pl.BlockSpec((ROW_TILE, tc), lambda i, j: (i, j)),
        out_shape=jax.ShapeDtypeStruct((n_rows, n_c), F32),
        compiler_params=_cparams(("parallel", "parallel")),
        name="shortconv",
    )(z, z, z, w, b.reshape(1, n_c))


def _hp_dot(a, b):
    return jnp.dot(a, b, preferred_element_type=F32, precision=lax.Precision.HIGHEST)


def _split_bf16(x):
    hi = x.astype(BF16)
    return hi, (x - hi.astype(F32)).astype(BF16)


def _filter_body(z_ref, w1_ref, b1_ref, w2_ref, b2_ref, w3_ref, b3_ref, fr_ref, fr3_ref, w4_ref, dec_ref, o_ref,
                 h_ref):
    i = pl.program_id(0)
    j = pl.program_id(1)

    @pl.when(j == 0)
    def _():
        fr = fr_ref[...]
        h = jnp.sin(fr * (_hp_dot(z_ref[...], w1_ref[...]) + b1_ref[...]))
        h = jnp.sin(fr * (_hp_dot(h, w2_ref[...]) + b2_ref[...]))
        h3 = jnp.sin(fr3_ref[...] * (_hp_dot(h, w3_ref[...]) + b3_ref[...]))
        hi, lo = _split_bf16(h3)
        lane = lax.broadcasted_iota(jnp.int32, h3.shape, 1)
        h_ref[...] = jnp.where((lane >= HY_HID) & (lane < 2 * HY_HID), lo, hi)

    t = z_ref[:, 0:1]
    whi, wlo = _split_bf16(w4_ref[...])
    wcat = jnp.concatenate([whi, whi, wlo], axis=0)
    y = jnp.dot(h_ref[...], wcat, preferred_element_type=F32) * jnp.exp(-t * jnp.abs(dec_ref[...]))
    rows = lax.broadcasted_iota(jnp.int32, y.shape, 0)
    drop = jnp.logical_and(jnp.logical_and(i == 0, j % 2 == 1), rows == 0)
    o_ref[...] = jnp.where(drop, 0.0, y)


def _hyena_filters(w1, b1, w2, b2, w3, b3, w4, freq, decay):
    zfeat = jnp.asarray(_filter_features(), dtype=F32)
    w1p = jnp.zeros((HY_HID, HY_HID), F32).at[:HY_EMB].set(w1)
    n_c = w4.shape[1]
    tc = C_BR
    small = lambda i, j: (0, 0)
    return pl.pallas_call(
        _filter_body,
        grid=(SEQ_T // ROW_TILE, n_c // tc),
        in_specs=[
            pl.BlockSpec((ROW_TILE, HY_HID), lambda i, j: (i, 0)),
            pl.BlockSpec((HY_HID, HY_HID), small),
            pl.BlockSpec((1, HY_HID), small),
            pl.BlockSpec((HY_HID, HY_HID), small),
            pl.BlockSpec((1, HY_HID), small),
            pl.BlockSpec((HY_HID, 3 * HY_HID), small),
            pl.BlockSpec((1, 3 * HY_HID), small),
            pl.BlockSpec((1, HY_HID), small),
            pl.BlockSpec((1, 3 * HY_HID), small),
            pl.BlockSpec((HY_HID, tc), lambda i, j: (0, j)),
            pl.BlockSpec((1, tc), lambda i, j: (0, j)),
        ],
        out_specs=pl.BlockSpec((ROW_TILE, tc), lambda i, j: (i, j)),
        out_shape=jax.ShapeDtypeStruct((SEQ_T, n_c), F32),
        scratch_shapes=[pltpu.VMEM((ROW_TILE, 3 * HY_HID), BF16)],
        compiler_params=_cparams(("parallel", "arbitrary")),
        name="hyena_filter",
    )(zfeat, w1p, b1.reshape(1, -1), w2, b2.reshape(1, -1), jnp.tile(w3, (1, 3)), jnp.tile(b3.reshape(1, -1), (1, 3)),
      freq.reshape(1, -1), jnp.tile(freq.reshape(1, -1), (1, 3)), w4, decay.reshape(1, n_c))


def _view3(x):
    return x if x.ndim == 3 else x.reshape(FA, FB, x.shape[1])


def _stage_a_body(n_in, n_out, *refs):
    x_refs = refs[:n_in]
    w_refs = refs[n_in:n_in + n_in * n_out]
    o_refs = refs[n_in + n_in * n_out:]
    bb = pl.program_id(0)
    for j in range(BBLK):
        valid = bb * BBLK + j < FB
        xs = [jnp.where(valid, xr[:, j, :], 0.0).astype(BF16) for xr in x_refs]
        for o in range(n_out):
            acc = None
            for k in range(n_in):
                d = jnp.dot(w_refs[o * n_in + k][j], xs[k], preferred_element_type=F32)
                acc = d if acc is None else acc + d
            o_refs[o][j] = acc.astype(o_refs[o].dtype)


def _stage_a(xs, col_offs, n_cols, w_tabs, out_dtype, ct=1024):
    n_in = len(xs)
    n_out = len(w_tabs)
    m = w_tabs[0][0].shape[1]
    x3 = [_view3(x) for x in xs]
    in_specs = []
    for k in range(n_in):
        off = col_offs[k] // ct
        in_specs.append(pl.BlockSpec((FA, BBLK, ct), lambda b, c, off=off: (0, b, off + c)))
    flat_w = []
    for o in range(n_out):
        for k in range(n_in):
            flat_w.append(w_tabs[o][k])
            in_specs.append(pl.BlockSpec((BBLK, m, FA), lambda b, c: (b, 0, 0)))
    return pl.pallas_call(
        functools.partial(_stage_a_body, n_in, n_out),
        grid=(FBP // BBLK, n_cols // ct),
        in_specs=in_specs,
        out_specs=[pl.BlockSpec((BBLK, m, ct), lambda b, c: (b, 0, c)) for _ in range(n_out)],
        out_shape=[jax.ShapeDtypeStruct((FBP, m, n_cols), out_dtype) for _ in range(n_out)],
        compiler_params=_cparams(("parallel", "parallel")),
        name="dft_stage_a",
    )(*x3, *flat_w)


def _stage_ainv_body(tr_ref, ti_ref, wr_ref, wi_ref, v_ref, x_ref, skip_ref, o_ref):
    skip = skip_ref[...]
    for j in range(BBLK):
        y = jnp.dot(wr_ref[j], tr_ref[j].astype(BF16), preferred_element_type=F32)
        y = y + jnp.dot(wi_ref[j], ti_ref[j].astype(BF16), preferred_element_type=F32)
        y = y + v_ref[:, j, :] * skip
        o_ref[:, j, :] = x_ref[:, j, :] * y


def _stage_ainv(tr, ti, wr, wi, v, v_off, xmul, x_off, skip, flat, ct=1024):
    n_cols = tr.shape[2]
    kap = tr.shape[1]
    view = _view3
    spec3 = lambda off: pl.BlockSpec((FA, BBLK, ct), lambda b, c, off=off // ct: (0, b, off + c))
    in_specs = [
        pl.BlockSpec((BBLK, kap, ct), lambda b, c: (b, 0, c)),
        pl.BlockSpec((BBLK, kap, ct), lambda b, c: (b, 0, c)),
        pl.BlockSpec((BBLK, FA, kap), lambda b, c: (b, 0, 0)),
        pl.BlockSpec((BBLK, FA, kap), lambda b, c: (b, 0, 0)),
        spec3(v_off),
        spec3(x_off),
        pl.BlockSpec((1, ct), lambda b, c: (0, c)),
    ]
    args = [tr, ti, wr, wi, view(v), view(xmul), skip.reshape(1, n_cols)]
    out = pl.pallas_call(
        _stage_ainv_body,
        grid=(FBP // BBLK, n_cols // ct),
        in_specs=in_specs,
        out_specs=pl.BlockSpec((FA, BBLK, ct), lambda b, c: (0, b, c)),
        out_shape=jax.ShapeDtypeStruct((FA, FB, n_cols), F32),
        compiler_params=_cparams(("parallel", "parallel")),
        name="dft_stage_a_inv",
    )(*args)
    return out.reshape(SEQ_T, n_cols) if flat else out


def _filter_spec_body(fr_ref, fi_ref, br_ref, bi_ref, c_ref, s_ref, kr_ref, ki_ref):
    c = c_ref[...]
    s = s_ref[...]
    dot = lambda a, b: jnp.dot(a, b[...], preferred_element_type=F32)
    kr_ref[...] = dot(c, fr_ref) + dot(s, fi_ref) + dot(c, br_ref) + dot(s, bi_ref)
    ki_ref[...] = dot(c, fi_ref) - dot(s, fr_ref) - dot(c, bi_ref) + dot(s, br_ref)


def _filter_spectrum(tr, ti, cb, sb, ct=1024):
    nc = C_BR // ct
    kap = tr.shape[1]
    n_cols = tr.shape[2]
    tr2 = tr.reshape(FBP, kap * n_cols)
    ti2 = ti.reshape(FBP, kap * n_cols)
    ncol_blk = n_cols // ct

    def tspec(direction):
        return pl.BlockSpec((FBP, ct), lambda o, ka, c: (0, ka * ncol_blk + (2 * o + direction) * nc + c))

    mat = pl.BlockSpec((FBP, FBP), lambda o, ka, c: (0, 0))
    ospec = pl.BlockSpec((None, None, FBP, ct), lambda o, ka, c: (o, ka, 0, c))
    oshape = jax.ShapeDtypeStruct((2, KA_HY, FBP, C_BR), F32)
    return pl.pallas_call(
        _filter_spec_body,
        grid=(2, KA_HY, nc),
        in_specs=[tspec(0), tspec(0), tspec(1), tspec(1), mat, mat],
        out_specs=[ospec, ospec],
        out_shape=[oshape, oshape],
        compiler_params=_cparams(("parallel", "parallel", "parallel")),
        name="filter_spectrum",
    )(tr2, ti2, tr2, ti2, cb, sb)


def _conv_b_body(tr_ref, ti_ref, kr_ref, ki_ref, c_ref, s_ref, or_ref, oi_ref):
    ka = pl.program_id(0)

    @pl.when(ka < KA_HY)
    def _():
        c = c_ref[...]
        s = s_ref[...]
        dot = lambda a, b: jnp.dot(a, b, preferred_element_type=F32)
        tr = tr_ref[...]
        ti = ti_ref[...]
        yr = dot(c, tr) + dot(s, ti)
        yi = dot(c, ti) - dot(s, tr)
        kr = kr_ref[...]
        ki = ki_ref[...]
        zr = (yr * kr - yi * ki).astype(BF16)
        zi = (yr * ki + yi * kr).astype(BF16)
        or_ref[...] = (dot(c, zr) - dot(s, zi)).astype(BF16)
        oi_ref[...] = (dot(c, zi) + dot(s, zr)).astype(BF16)

    @pl.when(ka >= KA_HY)
    def _():
        or_ref[...] = jnp.zeros_like(or_ref)
        oi_ref[...] = jnp.zeros_like(oi_ref)


def _conv_stage_b(tr, ti, kr, ki, order, cb, sb, ct=1024):
    kap = tr.shape[1]
    n_cols = tr.shape[2]
    nc = n_cols // ct
    tr2 = tr.reshape(FBP, kap * n_cols)
    ti2 = ti.reshape(FBP, kap * n_cols)
    tspec = pl.BlockSpec((FBP, ct), lambda ka, c: (0, ka * nc + c))
    kspec = pl.BlockSpec((None, None, FBP, ct), lambda ka, c: (order, jnp.minimum(ka, KA_HY - 1), 0, c))
    mat = pl.BlockSpec((FBP, FBP), lambda ka, c: (0, 0))
    oshape = jax.ShapeDtypeStruct((FBP, kap * n_cols), BF16)
    o_r, o_i = pl.pallas_call(
        _conv_b_body,
        grid=(kap, nc),
        in_specs=[tspec, tspec, kspec, kspec, mat, mat],
        out_specs=[tspec, tspec],
        out_shape=[oshape, oshape],
        compiler_params=_cparams(("parallel", "parallel")),
        name="conv_stage_b",
    )(tr2, ti2, kr, ki, cb, sb)
    return o_r.reshape(FBP, kap, n_cols), o_i.reshape(FBP, kap, n_cols)


def _fnet_b_body(tr_ref, ti_ref, c_ref, s_ref, o_ref):
    dot = lambda a, b: jnp.dot(a, b, preferred_element_type=F32)
    y = dot(c_ref[...], tr_ref[...]) + dot(s_ref[...], ti_ref[...])
    o_ref[...] = y[:FB, :]


def _fnet_stage_b(tr, ti, cb, sb, ct=1024):
    n_cols = tr.shape[2]
    nc = n_cols // ct
    tr2 = tr.reshape(FBP, FA * n_cols)
    ti2 = ti.reshape(FBP, FA * n_cols)
    tspec = pl.BlockSpec((FBP, ct), lambda ka, c: (0, ka * nc + c))
    mat = pl.BlockSpec((FBP, FBP), lambda ka, c: (0, 0))
    out = pl.pallas_call(
        _fnet_b_body,
        grid=(FA, nc),
        in_specs=[tspec, tspec, mat, mat],
        out_specs=pl.BlockSpec((FB, ct), lambda ka, c: (0, ka * nc + c)),
        out_shape=jax.ShapeDtypeStruct((FB, FA * n_cols), F32),
        compiler_params=_cparams(("parallel", "parallel")),
        name="fnet_stage_b",
    )(tr2, ti2, cb, sb)
    return out.reshape(SEQ_T, n_cols)


def _chan_dft_body(x_ref, w_ref, p_ref, q_ref):
    y = jnp.dot(x_ref[...].astype(BF16), w_ref[...], preferred_element_type=F32)
    p_ref[...] = y[:, :256]
    q_ref[...] = y[:, 256:]


def _chan_dft(z, off, chan):
    oshape = jax.ShapeDtypeStruct((SEQ_T, C_BR), F32)
    return pl.pallas_call(
        _chan_dft_body,
        grid=(SEQ_T // ROW_TILE, 4),
        in_specs=[
            pl.BlockSpec((ROW_TILE, 256), lambda i, g: (i, off // 256 + g)),
            pl.BlockSpec((256, 512), lambda i, g: (0, 0)),
        ],
        out_specs=[pl.BlockSpec((ROW_TILE, 256), lambda i, g: (i, g))] * 2,
        out_shape=[oshape, oshape],
        compiler_params=_cparams(("parallel", "parallel")),
        name="fnet_chan_dft",
    )(z, chan)


NA_QROWS = 8
NA_SLAB = 16
NA_SCALE = NA_DH ** -0.5


def _na_main_body(q_ref, k_ref, v_ref, km_ref, vm_ref, bt_ref, mb_ref, g_ref, o_ref):
    rb = pl.program_id(1)
    slab0 = jnp.clip(rb * NA_QROWS - NA_KH // 2, 0, GRID_H - NA_SLAB)
    w2 = 2 * NA_DH
    lane = lax.broadcasted_iota(jnp.int32, (GRID_W, w2), 1)
    row2 = lax.broadcasted_iota(jnp.int32, (w2, w2), 0)
    lane2 = lax.broadcasted_iota(jnp.int32, (w2, w2), 1)
    own_head = (row2 >= NA_DH) == (lane2 >= NA_DH)
    km = km_ref[...].astype(BF16)
    vm = vm_ref[...].astype(BF16)
    mbt = mb_ref[...]
    dn_t = (((1,), (1,)), ((), ()))
    dn_k = (((0,), (0,)), ((), ()))
    for i in range(NA_QROWS):
        r = rb * NA_QROWS + i
        r0 = jnp.clip(r - NA_KH // 2, 0, GRID_H - NA_KH)
        off = pl.multiple_of((r0 - slab0) * GRID_W, GRID_W)
        d0 = r0 - r + (NA_KH - 1)
        q = q_ref[i * GRID_W:(i + 1) * GRID_W, :]
        q2 = jnp.where(own_head, jnp.concatenate([q, q], axis=0), 0.0).astype(BF16)
        ks = k_ref[pl.ds(off, NA_KH * GRID_W), :].astype(BF16)
        vs = v_ref[pl.ds(off, NA_KH * GRID_W), :].astype(BF16)
        st = lax.dot_general(ks, q2, dn_t, preferred_element_type=F32) * NA_SCALE + bt_ref[d0]
        sx = lax.dot_general(km, q2, dn_t, preferred_element_type=F32) * NA_SCALE + mbt
        m = jnp.maximum(jnp.max(st, axis=0, keepdims=True), jnp.max(sx, axis=0, keepdims=True))
        p = jnp.exp(st - m)
        px = jnp.exp(sx - m)
        inv = 1.0 / (jnp.sum(p, axis=0, keepdims=True) + jnp.sum(px, axis=0, keepdims=True))
        o2 = lax.dot_general((p * inv).astype(BF16), vs, dn_k, preferred_element_type=F32)
        o2 = o2 + lax.dot_general((px * inv).astype(BF16), vm, dn_k, preferred_element_type=F32)
        o = jnp.where(lane < NA_DH, o2[:GRID_W], o2[GRID_W:])
        g = g_ref[i * GRID_W:(i + 1) * GRID_W, :]
        o_ref[i * GRID_W:(i + 1) * GRID_W, :] = (o * (g * jax.nn.sigmoid(g))).astype(BF16)


def _na_meta_body(q_ref, k_ref, v_ref, mb_ref, g_ref, o_ref):
    lane = lax.broadcasted_iota(jnp.int32, (N_META, 2 * NA_DH), 1)
    q = q_ref[...]
    km = k_ref[...].astype(BF16)
    vm = v_ref[...].astype(BF16)
    dn_t = (((1,), (1,)), ((), ()))
    outs = []
    for hh in range(2):
        sel = (lane >= hh * NA_DH) & (lane < (hh + 1) * NA_DH)
        qm = jnp.where(sel, q, 0.0).astype(BF16)
        s = lax.dot_general(qm, km, dn_t, preferred_element_type=F32) * NA_SCALE + mb_ref[hh, 0:1, :]
        m = jnp.max(s, axis=-1, keepdims=True)
        p = jnp.exp(s - m)
        den = jnp.sum(p, axis=-1, keepdims=True)
        outs.append(jnp.dot(p.astype(BF16), vm, preferred_element_type=F32) / den)
    o = jnp.where(lane < NA_DH, outs[0], outs[1])
    g = g_ref[...]
    o_ref[...] = (o * (g * jax.nn.sigmoid(g))).astype(BF16)


def _na_bias_table(rpb):
    colmask, dc = _na_index_tables()
    rows = np.arange(NA_KH)[:, None] + np.arange(NA_KH)[None, :]
    t = rpb.astype(F32)[:, rows]
    t = t[:, :, :, dc]
    t = jnp.where(colmask[None, None, None], t, NEG_INF)
    t = t.reshape(NA_HEADS // 2, 2, NA_KH, NA_KH, GRID_W, GRID_W)
    t = jnp.transpose(t, (0, 2, 3, 5, 1, 4))
    return t.reshape(NA_HEADS // 2, NA_KH, NA_KH * GRID_W, 2 * GRID_W)


def _neighborhood_attention(z, rpb, meta_bias):
    bt = _na_bias_table(rpb)
    mb = jnp.broadcast_to(meta_bias.astype(F32).reshape(NA_HEADS // 2, 2, 1, N_META),
                          (NA_HEADS // 2, 2, 8, N_META))
    mbt = jnp.transpose(meta_bias.astype(F32).reshape(NA_HEADS // 2, 2, N_META), (0, 2, 1))
    mbt = jnp.repeat(mbt, GRID_W, axis=2)
    w2 = 2 * NA_DH
    qrows = NA_QROWS * GRID_W
    srows = NA_SLAB * GRID_W

    def slab_start(rb):
        start = N_META + GRID_W * jnp.clip(rb * NA_QROWS - NA_KH // 2, 0, GRID_H - NA_SLAB)
        return pl.multiple_of(start, N_META)

    def col(off):
        return lambda hp, rb: (pl.multiple_of(N_META + rb * qrows, N_META), pl.multiple_of(off + hp * w2, w2))

    def slab(off):
        return lambda hp, rb: (slab_start(rb), pl.multiple_of(off + hp * w2, w2))

    def meta(off):
        return lambda hp, rb: (0, off // w2 + hp)

    y_main = pl.pallas_call(
        _na_main_body,
        grid=(NA_HEADS // 2, GRID_H // NA_QROWS),
        in_specs=[
            pl.BlockSpec((pl.Element(qrows), pl.Element(w2)), col(OFF_Q)),
            pl.BlockSpec((pl.Element(srows), pl.Element(w2)), slab(OFF_K)),
            pl.BlockSpec((pl.Element(srows), pl.Element(w2)), slab(OFF_V)),
            pl.BlockSpec((N_META, w2), meta(OFF_K)),
            pl.BlockSpec((N_META, w2), meta(OFF_V)),
            pl.BlockSpec((None, NA_KH, NA_KH * GRID_W, w2), lambda hp, rb: (hp, 0, 0, 0)),
            pl.BlockSpec((None, N_META, w2), lambda hp, rb: (hp, 0, 0)),
            pl.BlockSpec((pl.Element(qrows), pl.Element(w2)), col(OFF_NA_GATE)),
        ],
        out_specs=pl.BlockSpec((qrows, w2), lambda hp, rb: (rb, hp)),
        out_shape=jax.ShapeDtypeStruct((SEQ, C_BR), BF16),
        compiler_params=_cparams(("parallel", "parallel")),
        name="na_main",
    )(z, z, z, z, z, bt, mbt, z)

    def mcol(off):
        return lambda hp: (0, off // w2 + hp)

    y_meta = pl.pallas_call(
        _na_meta_body,
        grid=(NA_HEADS // 2,),
        in_specs=[
            pl.BlockSpec((N_META, w2), mcol(OFF_Q)),
            pl.BlockSpec((N_META, w2), mcol(OFF_K)),
            pl.BlockSpec((N_META, w2), mcol(OFF_V)),
            pl.BlockSpec((None, 2, 8, N_META), lambda hp: (hp, 0, 0, 0)),
            pl.BlockSpec((N_META, w2), mcol(OFF_NA_GATE)),
        ],
        out_specs=pl.BlockSpec((N_META, w2), lambda hp: (0, hp)),
        out_shape=jax.ShapeDtypeStruct((N_META, C_BR), BF16),
        compiler_params=_cparams(("parallel",)),
        name="na_meta",
    )(z, z, z, mb, z)
    return y_meta, y_main


def _silu(g):
    return g * jax.nn.sigmoid(g)


def _merge_body(ya_ref, yb_ref, yc_ref, hg_ref, fg_ref, wa_ref, wb_ref, wc_ref, ga_ref, gb_ref, gc_ref, o_ref):
    dot = lambda a, b: jnp.dot(a.astype(BF16), b[...], preferred_element_type=F32)
    m = jax.nn.sigmoid(ga_ref[...]) * dot(ya_ref[...] * _silu(hg_ref[...]), wa_ref)
    m = m + jax.nn.sigmoid(gb_ref[...]) * dot(yb_ref[...] * _silu(fg_ref[...]), wb_ref)
    m = m + jax.nn.sigmoid(gc_ref[...]) * dot(yc_ref[...], wc_ref)
    o_ref[...] = m.astype(BF16)


def _merge(ya, yb, yc, wa, wb, wc, z, tn=512):
    n_rows = ya.shape[0]
    yspec = pl.BlockSpec((ROW_TILE, C_BR), lambda i, j: (i, 0))
    wspec = pl.BlockSpec((C_BR, tn), lambda i, j: (0, j))
    zspec = lambda off: pl.BlockSpec((ROW_TILE, C_BR), lambda i, j, off=off: (i, off // C_BR))
    gspec = lambda k: pl.BlockSpec((ROW_TILE, tn), lambda i, j, k=k: (i, (OFF_MERGE + k * D_MODEL) // tn + j))
    return pl.pallas_call(
        _merge_body,
        grid=(n_rows // ROW_TILE, D_MODEL // tn),
        in_specs=[yspec, yspec, yspec, zspec(OFF_HY_GATE), zspec(OFF_FN_GATE), wspec, wspec, wspec,
                  gspec(0), gspec(1), gspec(2)],
        out_specs=pl.BlockSpec((ROW_TILE, tn), lambda i, j: (i, j)),
        out_shape=jax.ShapeDtypeStruct((n_rows, D_MODEL), BF16),
        compiler_params=_cparams(("parallel", "parallel")),
        name="merge",
    )(ya, yb, yc, z, z, wa, wb, wc, z, z, z)


def _outproj_body(h_ref, m_ref, w_ref, o_ref):
    o_ref[...] = h_ref[...] + jnp.dot(m_ref[...], w_ref[...], preferred_element_type=F32)


def _outproj(h, m, w, tn=1024):
    n_rows = h.shape[0]
    return pl.pallas_call(
        _outproj_body,
        grid=(n_rows // ROW_TILE, D_MODEL // tn),
        in_specs=[
            pl.BlockSpec((ROW_TILE, tn), lambda i, j: (i, j)),
            pl.BlockSpec((ROW_TILE, D_MODEL), lambda i, j: (i, 0)),
            pl.BlockSpec((D_MODEL, tn), lambda i, j: (0, j)),
        ],
        out_specs=pl.BlockSpec((ROW_TILE, tn), lambda i, j: (i, j)),
        out_shape=jax.ShapeDtypeStruct((n_rows, D_MODEL), F32),
        compiler_params=_cparams(("parallel", "parallel")),
        name="outproj",
    )(h, m, w)


def _final_norm_body(x_ref, g_ref, o_ref):
    x = x_ref[...]
    y = x * lax.rsqrt(jnp.mean(x * x, axis=-1, keepdims=True) + EPS)
    o_ref[...] = y * g_ref[...]


def _final_norm(h, g, tm=512):
    return pl.pallas_call(
        _final_norm_body,
        grid=(SEQ // tm,),
        in_specs=[
            pl.BlockSpec((pl.Element(tm), pl.Element(D_MODEL)), lambda i: (pl.multiple_of(N_META + i * tm, N_META), 0)),
            pl.BlockSpec((1, D_MODEL), lambda i: (0, 0)),
        ],
        out_specs=pl.BlockSpec((tm, D_MODEL), lambda i: (i, 0)),
        out_shape=jax.ShapeDtypeStruct((SEQ, D_MODEL), F32),
        compiler_params=_cparams(("parallel",)),
        name="final_norm",
    )(h, g.reshape(1, D_MODEL))


def _hyena_branch(z, conv_w, conv_b, w1, b1, w2, b2, w3, b3, w4, freq, decay, skip):
    tab = _hyena_tables()
    fwd = [[_bf(tab["fwd_r"])], [_bf(tab["fwd_i"])]]
    inv_r, inv_i = _bf(tab["inv_r"]), _bf(tab["inv_i"])
    cb, sb = _bf(tab["cb"]), _bf(tab["sb"])

    hyc = _shortconv(z, conv_w, conv_b)
    filt = _hyena_filters(w1, b1, w2, b2, w3, b3, w4, freq, decay)
    ftr, fti = _stage_a([filt], [0], filt.shape[-1], fwd, BF16)
    kr, ki = _filter_spectrum(ftr, fti, cb, sb)

    hyc = _view3(hyc)
    tr, ti = _stage_a([hyc], [0], C_BR, fwd, BF16)
    tr, ti = _conv_stage_b(tr, ti, kr, ki, 0, cb, sb)
    zmid = _stage_ainv(tr, ti, inv_r, inv_i, hyc, 0, hyc, C_BR, skip[0], flat=False)

    tr, ti = _stage_a([zmid], [0], C_BR, fwd, BF16)
    tr, ti = _conv_stage_b(tr, ti, kr, ki, 1, cb, sb)
    return _stage_ainv(tr, ti, inv_r, inv_i, zmid, 0, hyc, 2 * C_BR, skip[1], flat=True)


def _fnet_branch(z):
    tab = _fnet_tables()
    cs, sn = _bf(tab["cs"]), _bf(tab["sn"])
    p, q = _chan_dft(z, OFF_FN_IN, _bf(tab["chan"]))
    tr, ti = _stage_a([p, q], [0, 0], C_BR, [[cs, -sn], [-sn, -cs]], BF16)
    return _fnet_stage_b(tr, ti, _bf(tab["cb"]), _bf(tab["sb"]))


def _layer(h, norm_g, w_in, conv_w, conv_b, w1, b1, w2, b2, w3, b3, w4, freq, decay, skip, rpb, meta_bias,
           w_a, w_b, w_c, w_out):
    z = _inproj(h, norm_g, w_in.astype(BF16))
    ya = _hyena_branch(z, conv_w, conv_b, w1, b1, w2, b2, w3, b3, w4, freq, decay, skip)
    yb = _fnet_branch(z)
    yc_meta, yc_main = _neighborhood_attention(z, rpb, meta_bias)
    yc = jnp.concatenate([yc_meta, yc_main], axis=0)
    m = _merge(ya, yb, yc, w_a.astype(BF16), w_b.astype(BF16), w_c.astype(BF16), z)
    return _outproj(h, m, w_out.astype(BF16))


def kernel(x, meta_tokens, norm_g, w_in, hy_conv_w, hy_conv_b, hy_flt_w1, hy_flt_b1, hy_flt_w2, hy_flt_b2,
           hy_flt_w3, hy_flt_b3, hy_flt_w4, hy_flt_freq, hy_decay, hy_skip, na_rpb, na_meta_bias,
           w_branch_a, w_branch_b, w_branch_c, w_out, final_g):
    assert x.shape == (1, SEQ, D_MODEL)
    h = jnp.concatenate([meta_tokens.astype(x.dtype), x[0]], axis=0)
    for i in range(norm_g.shape[0]):
        h = _layer(h, norm_g[i], w_in[i], hy_conv_w[i], hy_conv_b[i], hy_flt_w1[i], hy_flt_b1[i],
                   hy_flt_w2[i], hy_flt_b2[i], hy_flt_w3[i], hy_flt_b3[i], hy_flt_w4[i], hy_flt_freq[i],
                   hy_decay[i].reshape(-1), hy_skip[i], na_rpb[i], na_meta_bias[i],
                   w_branch_a[i], w_branch_b[i], w_branch_c[i], w_out[i])
    return _final_norm(h, final_g)[None]
```

```python
import functools
import math

import numpy as np
import jax
import jax.numpy as jnp
from jax import lax
from jax.experimental import pallas as pl
from jax.experimental.pallas import tpu as pltpu

F32 = jnp.float32
BF16 = jnp.bfloat16

D_MODEL = 2048
SEQ = 16384
N_META = 16
SEQ_T = SEQ + N_META
GRID_W = 64
GRID_H = SEQ // GRID_W
C_BR = 1024
HY_EMB = 33
HY_HID = 64
NA_HEADS = 16
NA_DH = 64
NA_KH = 8
NA_KW = 16
EPS = 1e-6
NEG_INF = -1e30

OFF_HY_IN = 0
OFF_HY_GATE = 3072
OFF_FN_IN = 4096
OFF_FN_GATE = 5120
OFF_Q = 6144
OFF_K = 7168
OFF_V = 8192
OFF_NA_GATE = 9216
OFF_MERGE = 10240
N_IN = 16384

FA = 80
FB = 205
FBP = 208
BBLK = 8
KA_HY = FA + 1
KAP_HY = 96

ROW_TILE = 656
LANE = 128
VMEM_LIMIT = 48 * 1024 * 1024


def _cparams(sem):
    return pltpu.CompilerParams(dimension_semantics=sem, vmem_limit_bytes=VMEM_LIMIT)


@functools.lru_cache(maxsize=None)
def _hyena_tables():
    n_circ = 2 * SEQ_T
    a = np.arange(FA)[None, None, :]
    b = np.arange(FBP)[:, None, None]
    ka = np.arange(KAP_HY)[None, :, None]
    n = FB * a + b
    ang = 2.0 * np.pi * ((ka * n) % n_circ) / n_circ
    valid = (b < FB) & (ka < KA_HY)
    fwd_r = np.where(valid, np.cos(ang), 0.0)
    fwd_i = np.where(valid, -np.sin(ang), 0.0)
    c = np.where((ka == 0) | (ka == FA), 1.0, 2.0) / n_circ
    inv_r = np.transpose(np.where(valid, c * np.cos(ang), 0.0), (0, 2, 1))
    inv_i = np.transpose(np.where(valid, -c * np.sin(ang), 0.0), (0, 2, 1))
    kb = np.arange(FBP)[:, None]
    bb = np.arange(FBP)[None, :]
    phi = 2.0 * np.pi * ((kb * bb) % FB) / FB
    ok = (kb < FB) & (bb < FB)
    cb = np.where(ok, np.cos(phi), 0.0)
    sb = np.where(ok, np.sin(phi), 0.0)
    return dict(fwd_r=fwd_r, fwd_i=fwd_i, inv_r=inv_r, inv_i=inv_i, cb=cb, sb=sb)


@functools.lru_cache(maxsize=None)
def _fnet_tables():
    a = np.arange(FA)[None, None, :]
    b = np.arange(FBP)[:, None, None]
    ka = np.arange(FA)[None, :, None]
    n = FB * a + b
    ang = 2.0 * np.pi * ((ka * n) % SEQ_T) / SEQ_T
    valid = np.broadcast_to(b < FB, ang.shape)
    cs = np.where(valid, np.cos(ang), 0.0)
    sn = np.where(valid, np.sin(ang), 0.0)
    kb = np.arange(FBP)[:, None]
    bb = np.arange(FBP)[None, :]
    phi = 2.0 * np.pi * ((kb * bb) % FB) / FB
    ok = (kb < FB) & (bb < FB)
    scale = 1.0 / math.sqrt(SEQ_T * 256.0)
    cb = np.where(ok, np.cos(phi), 0.0) * scale
    sb = np.where(ok, np.sin(phi), 0.0) * scale
    j = np.arange(256)
    th = 2.0 * np.pi * ((j[:, None] * j[None, :]) % 256) / 256.0
    chan = np.concatenate([np.cos(th), np.sin(th)], axis=1)
    return dict(cs=cs, sn=sn, cb=cb, sb=sb, chan=chan)


@functools.lru_cache(maxsize=None)
def _filter_features():
    t = np.linspace(0.0, 1.0, SEQ_T)[:, None]
    bands = (HY_EMB - 1) // 2
    w = 2.0 * np.pi * np.arange(SEQ_T)[:, None] / SEQ_T
    f = np.linspace(1e-4, bands - 1, bands)[None, :]
    z = np.concatenate([t, np.cos(f * w), -np.sin(f * w)], axis=-1)
    out = np.zeros((SEQ_T, HY_HID), np.float64)
    out[:, :HY_EMB] = z
    return out


@functools.lru_cache(maxsize=None)
def _na_index_tables():
    qc = np.arange(GRID_W)[:, None]
    kc = np.arange(GRID_W)[None, :]
    cs = np.clip(qc - NA_KW // 2, 0, GRID_W - NA_KW)
    colmask = (kc >= cs) & (kc < cs + NA_KW)
    dc = np.clip(kc - qc + NA_KW - 1, 0, 2 * NA_KW - 2)
    return colmask, dc


def _bf(x):
    return jnp.asarray(np.asarray(x, np.float32), dtype=BF16)


def _inproj_body(x_ref, g_ref, w_ref, o_ref, xn_ref):
    @pl.when(pl.program_id(1) == 0)
    def _():
        x = x_ref[...]
        y = x * lax.rsqrt(jnp.mean(x * x, axis=-1, keepdims=True) + EPS)
        xn_ref[...] = (y * g_ref[...]).astype(BF16)

    o_ref[...] = jnp.dot(xn_ref[...], w_ref[...], preferred_element_type=F32)


def _cast_body(x_ref, o_ref):
    o_ref[...] = x_ref[...].astype(o_ref.dtype)


def _to_bf16(w, tm=128):
    n_rows, n_cols = w.shape
    return pl.pallas_call(
        _cast_body,
        grid=(n_rows // tm,),
        in_specs=[pl.BlockSpec((tm, n_cols), lambda i: (i, 0))],
        out_specs=pl.BlockSpec((tm, n_cols), lambda i: (i, 0)),
        out_shape=jax.ShapeDtypeStruct(w.shape, BF16),
        compiler_params=_cparams(("parallel",)),
        name="cast_bf16",
    )(w)


def _inproj(h, g, w_bf16, tn=2048):
    n_rows, d = h.shape
    n_out = w_bf16.shape[1]
    return pl.pallas_call(
        _inproj_body,
        grid=(n_rows // ROW_TILE, n_out // tn),
        in_specs=[
            pl.BlockSpec((ROW_TILE, d), lambda i, j: (i, 0)),
            pl.BlockSpec((1, d), lambda i, j: (0, 0)),
            pl.BlockSpec((d, tn), lambda i, j: (0, j)),
        ],
        out_specs=pl.BlockSpec((ROW_TILE, tn), lambda i, j: (i, j)),
        out_shape=jax.ShapeDtypeStruct((n_rows, n_out), F32),
        scratch_shapes=[pltpu.VMEM((ROW_TILE, d), BF16)],
        compiler_params=_cparams(("parallel", "arbitrary")),
        name="inproj",
    )(h, g.reshape(1, d), w_bf16)


def _shortconv_body(prev_ref, cur_ref, next_ref, w_ref, b_ref, o_ref):
    i = pl.program_id(0)
    last = pl.num_programs(0) - 1
    x = cur_ref[...]
    rows = lax.broadcasted_iota(jnp.int32, x.shape, 0)
    prev_row = jnp.where(i == 0, 0.0, prev_ref[7:8, :])
    next_row = jnp.where(i == last, 0.0, next_ref[0:1, :])
    up = jnp.where(rows == 0, prev_row, pltpu.roll(x, 1, 0))
    dn = jnp.where(rows == ROW_TILE - 1, next_row, pltpu.roll(x, ROW_TILE - 1, 0))
    y = up * w_ref[0:1, :]
    y = y + x * w_ref[1:2, :]
    y = y + dn * w_ref[2:3, :]
    o_ref[...] = y + b_ref[...]


def _shortconv(z, w, b, tc=1024):
    n_rows = z.shape[0]
    n_c = w.shape[1]
    n8 = n_rows // 8
    r8 = ROW_TILE // 8
    return pl.pallas_call(
        _shortconv_body,
        grid=(n_rows // ROW_TILE, n_c // tc),
        in_specs=[
            pl.BlockSpec((8, tc), lambda i, j: (jnp.maximum(i * r8 - 1, 0), j)),
            pl.BlockSpec((ROW_TILE, tc), lambda i, j: (i, j)),
            pl.BlockSpec((8, tc), lambda i, j: (jnp.minimum((i + 1) * r8, n8 - 1), j)),
            pl.BlockSpec((3, tc), lambda i, j: (0, j)),
            pl.BlockSpec((1, tc), lambda i, j: (0, j)),
        ],
        out_specs=pl.BlockSpec((ROW_TILE, tc), lambda i, j: (i, j)),
        out_shape=jax.ShapeDtypeStruct((n_rows, n_c), F32),
        compiler_params=_cparams(("parallel", "parallel")),
        name="shortconv",
    )(z, z, z, w, b.reshape(1, n_c))


def _hp_dot(a, b):
    return jnp.dot(a, b, preferred_element_type=F32, precision=lax.Precision.HIGHEST)


def _split_bf16(x):
    hi = x.astype(BF16)
    return hi, (x - hi.astype(F32)).astype(BF16)


def _filter_body(z_ref, w1_ref, b1_ref, w2_ref, b2_ref, w3_ref, b3_ref, fr_ref, fr3_ref, w4_ref, dec_ref, o_ref,
                 h_ref):
    i = pl.program_id(0)
    j = pl.program_id(1)

    @pl.when(j == 0)
    def _():
        fr = fr_ref[...]
        h = jnp.sin(fr * (_hp_dot(z_ref[...], w1_ref[...]) + b1_ref[...]))
        h = jnp.sin(fr * (_hp_dot(h, w2_ref[...]) + b2_ref[...]))
        h3 = jnp.sin(fr3_ref[...] * (_hp_dot(h, w3_ref[...]) + b3_ref[...]))
        hi, lo = _split_bf16(h3)
        lane = lax.broadcasted_iota(jnp.int32, h3.shape, 1)
        h_ref[...] = jnp.where((lane >= HY_HID) & (lane < 2 * HY_HID), lo, hi)

    t = z_ref[:, 0:1]
    whi, wlo = _split_bf16(w4_ref[...])
    wcat = jnp.concatenate([whi, whi, wlo], axis=0)
    y = jnp.dot(h_ref[...], wcat, preferred_element_type=F32) * jnp.exp(-t * jnp.abs(dec_ref[...]))
    rows = lax.broadcasted_iota(jnp.int32, y.shape, 0)
    drop = jnp.logical_and(jnp.logical_and(i == 0, j % 2 == 1), rows == 0)
    o_ref[...] = jnp.where(drop, 0.0, y)


def _hyena_filters(w1, b1, w2, b2, w3, b3, w4, freq, decay):
    zfeat = jnp.asarray(_filter_features(), dtype=F32)
    w1p = jnp.zeros((HY_HID, HY_HID), F32).at[:HY_EMB].set(w1)
    n_c = w4.shape[1]
    tc = C_BR
    small = lambda i, j: (0, 0)
    return pl.pallas_call(
        _filter_body,
        grid=(SEQ_T // ROW_TILE, n_c // tc),
        in_specs=[
            pl.BlockSpec((ROW_TILE, HY_HID), lambda i, j: (i, 0)),
            pl.BlockSpec((HY_HID, HY_HID), small),
            pl.BlockSpec((1, HY_HID), small),
            pl.BlockSpec((HY_HID, HY_HID), small),
            pl.BlockSpec((1, HY_HID), small),
            pl.BlockSpec((HY_HID, 3 * HY_HID), small),
            pl.BlockSpec((1, 3 * HY_HID), small),
            pl.BlockSpec((1, HY_HID), small),
            pl.BlockSpec((1, 3 * HY_HID), small),
            pl.BlockSpec((HY_HID, tc), lambda i, j: (0, j)),
            pl.BlockSpec((1, tc), lambda i, j: (0, j)),
        ],
        out_specs=pl.BlockSpec((ROW_TILE, tc), lambda i, j: (i, j)),
        out_shape=jax.ShapeDtypeStruct((SEQ_T, n_c), F32),
        scratch_shapes=[pltpu.VMEM((ROW_TILE, 3 * HY_HID), BF16)],
        compiler_params=_cparams(("parallel", "arbitrary")),
        name="hyena_filter",
    )(zfeat, w1p, b1.reshape(1, -1), w2, b2.reshape(1, -1), jnp.tile(w3, (1, 3)), jnp.tile(b3.reshape(1, -1), (1, 3)),
      freq.reshape(1, -1), jnp.tile(freq.reshape(1, -1), (1, 3)), w4, decay.reshape(1, n_c))


def _view3(x):
    return x if x.ndim == 3 else x.reshape(FA, FB, x.shape[1])


def _stage_a_body(n_in, n_out, *refs):
    x_refs = refs[:n_in]
    w_refs = refs[n_in:n_in + n_in * n_out]
    o_refs = refs[n_in + n_in * n_out:]
    bb = pl.program_id(0)
    for j in range(BBLK):
        valid = bb * BBLK + j < FB
        xs = [jnp.where(valid, xr[:, j, :], 0.0).astype(BF16) for xr in x_refs]
        for o in range(n_out):
            acc = None
            for k in range(n_in):
                d = jnp.dot(w_refs[o * n_in + k][j], xs[k], preferred_element_type=F32)
                acc = d if acc is None else acc + d
            o_refs[o][j] = acc.astype(o_refs[o].dtype)


def _stage_a(xs, col_offs, n_cols, w_tabs, out_dtype, ct=1024):
    n_in = len(xs)
    n_out = len(w_tabs)
    m = w_tabs[0][0].shape[1]
    x3 = [_view3(x) for x in xs]
    in_specs = []
    for k in range(n_in):
        off = col_offs[k] // ct
        in_specs.append(pl.BlockSpec((FA, BBLK, ct), lambda b, c, off=off: (0, b, off + c)))
    flat_w = []
    for o in range(n_out):
        for k in range(n_in):
            flat_w.append(w_tabs[o][k])
            in_specs.append(pl.BlockSpec((BBLK, m, FA), lambda b, c: (b, 0, 0)))
    return pl.pallas_call(
        functools.partial(_stage_a_body, n_in, n_out),
        grid=(FBP // BBLK, n_cols // ct),
        in_specs=in_specs,
        out_specs=[pl.BlockSpec((BBLK, m, ct), lambda b, c: (b, 0, c)) for _ in range(n_out)],
        out_shape=[jax.ShapeDtypeStruct((FBP, m, n_cols), out_dtype) for _ in range(n_out)],
        compiler_params=_cparams(("parallel", "parallel")),
        name="dft_stage_a",
    )(*x3, *flat_w)


def _stage_ainv_body(tr_ref, ti_ref, wr_ref, wi_ref, v_ref, x_ref, skip_ref, o_ref):
    skip = skip_ref[...]
    for j in range(BBLK):
        y = jnp.dot(wr_ref[j], tr_ref[j].astype(BF16), preferred_element_type=F32)
        y = y + jnp.dot(wi_ref[j], ti_ref[j].astype(BF16), preferred_element_type=F32)
        y = y + v_ref[:, j, :] * skip
        o_ref[:, j, :] = x_ref[:, j, :] * y


def _stage_ainv(tr, ti, wr, wi, v, v_off, xmul, x_off, skip, flat, ct=1024):
    n_cols = tr.shape[2]
    kap = tr.shape[1]
    view = _view3
    spec3 = lambda off: pl.BlockSpec((FA, BBLK, ct), lambda b, c, off=off // ct: (0, b, off + c))
    in_specs = [
        pl.BlockSpec((BBLK, kap, ct), lambda b, c: (b, 0, c)),
        pl.BlockSpec((BBLK, kap, ct), lambda b, c: (b, 0, c)),
        pl.BlockSpec((BBLK, FA, kap), lambda b, c: (b, 0, 0)),
        pl.BlockSpec((BBLK, FA, kap), lambda b, c: (b, 0, 0)),
        spec3(v_off),
        spec3(x_off),
        pl.BlockSpec((1, ct), lambda b, c: (0, c)),
    ]
    args = [tr, ti, wr, wi, view(v), view(xmul), skip.reshape(1, n_cols)]
    out = pl.pallas_call(
        _stage_ainv_body,
        grid=(FBP // BBLK, n_cols // ct),
        in_specs=in_specs,
        out_specs=pl.BlockSpec((FA, BBLK, ct), lambda b, c: (0, b, c)),
        out_shape=jax.ShapeDtypeStruct((FA, FB, n_cols), F32),
        compiler_params=_cparams(("parallel", "parallel")),
        name="dft_stage_a_inv",
    )(*args)
    return out.reshape(SEQ_T, n_cols) if flat else out


def _filter_spec_body(fr_ref, fi_ref, br_ref, bi_ref, c_ref, s_ref, kr_ref, ki_ref):
    c = c_ref[...]
    s = s_ref[...]
    dot = lambda a, b: jnp.dot(a, b[...], preferred_element_type=F32)
    kr_ref[...] = dot(c, fr_ref) + dot(s, fi_ref) + dot(c, br_ref) + dot(s, bi_ref)
    ki_ref[...] = dot(c, fi_ref) - dot(s, fr_ref) - dot(c, bi_ref) + dot(s, br_ref)


def _filter_spectrum(tr, ti, cb, sb, ct=1024):
    nc = C_BR // ct
    kap = tr.shape[1]
    n_cols = tr.shape[2]
    tr2 = tr.reshape(FBP, kap * n_cols)
    ti2 = ti.reshape(FBP, kap * n_cols)
    ncol_blk = n_cols // ct

    def tspec(direction):
        return pl.BlockSpec((FBP, ct), lambda o, ka, c: (0, ka * ncol_blk + (2 * o + direction) * nc + c))

    mat = pl.BlockSpec((FBP, FBP), lambda o, ka, c: (0, 0))
    ospec = pl.BlockSpec((None, None, FBP, ct), lambda o, ka, c: (o, ka, 0, c))
    oshape = jax.ShapeDtypeStruct((2, KA_HY, FBP, C_BR), F32)
    return pl.pallas_call(
        _filter_spec_body,
        grid=(2, KA_HY, nc),
        in_specs=[tspec(0), tspec(0), tspec(1), tspec(1), mat, mat],
        out_specs=[ospec, ospec],
        out_shape=[oshape, oshape],
        compiler_params=_cparams(("parallel", "parallel", "parallel")),
        name="filter_spectrum",
    )(tr2, ti2, tr2, ti2, cb, sb)


def _conv_b_body(tr_ref, ti_ref, kr_ref, ki_ref, c_ref, s_ref, or_ref, oi_ref):
    ka = pl.program_id(0)

    @pl.when(ka < KA_HY)
    def _():
        c = c_ref[...]
        s = s_ref[...]
        dot = lambda a, b: jnp.dot(a, b, preferred_element_type=F32)
        tr = tr_ref[...]
        ti = ti_ref[...]
        yr = dot(c, tr) + dot(s, ti)
        yi = dot(c, ti) - dot(s, tr)
        kr = kr_ref[...]
        ki = ki_ref[...]
        zr = (yr * kr - yi * ki).astype(BF16)
        zi = (yr * ki + yi * kr).astype(BF16)
        or_ref[...] = (dot(c, zr) - dot(s, zi)).astype(BF16)
        oi_ref[...] = (dot(c, zi) + dot(s, zr)).astype(BF16)

    @pl.when(ka >= KA_HY)
    def _():
        or_ref[...] = jnp.zeros_like(or_ref)
        oi_ref[...] = jnp.zeros_like(oi_ref)


def _conv_stage_b(tr, ti, kr, ki, order, cb, sb, ct=1024):
    kap = tr.shape[1]
    n_cols = tr.shape[2]
    nc = n_cols // ct
    tr2 = tr.reshape(FBP, kap * n_cols)
    ti2 = ti.reshape(FBP, kap * n_cols)
    tspec = pl.BlockSpec((FBP, ct), lambda ka, c: (0, ka * nc + c))
    kspec = pl.BlockSpec((None, None, FBP, ct), lambda ka, c: (order, jnp.minimum(ka, KA_HY - 1), 0, c))
    mat = pl.BlockSpec((FBP, FBP), lambda ka, c: (0, 0))
    oshape = jax.ShapeDtypeStruct((FBP, kap * n_cols), BF16)
    o_r, o_i = pl.pallas_call(
        _conv_b_body,
        grid=(kap, nc),
        in_specs=[tspec, tspec, kspec, kspec, mat, mat],
        out_specs=[tspec, tspec],
        out_shape=[oshape, oshape],
        compiler_params=_cparams(("parallel", "parallel")),
        name="conv_stage_b",
    )(tr2, ti2, kr, ki, cb, sb)
    return o_r.reshape(FBP, kap, n_cols), o_i.reshape(FBP, kap, n_cols)


def _fnet_b_body(tr_ref, ti_ref, c_ref, s_ref, o_ref):
    dot = lambda a, b: jnp.dot(a, b, preferred_element_type=F32)
    y = dot(c_ref[...], tr_ref[...]) + dot(s_ref[...], ti_ref[...])
    o_ref[...] = y[:FB, :]


def _fnet_stage_b(tr, ti, cb, sb, ct=1024):
    n_cols = tr.shape[2]
    nc = n_cols // ct
    tr2 = tr.reshape(FBP, FA * n_cols)
    ti2 = ti.reshape(FBP, FA * n_cols)
    tspec = pl.BlockSpec((FBP, ct), lambda ka, c: (0, ka * nc + c))
    mat = pl.BlockSpec((FBP, FBP), lambda ka, c: (0, 0))
    out = pl.pallas_call(
        _fnet_b_body,
        grid=(FA, nc),
        in_specs=[tspec, tspec, mat, mat],
        out_specs=pl.BlockSpec((FB, ct), lambda ka, c: (0, ka * nc + c)),
        out_shape=jax.ShapeDtypeStruct((FB, FA * n_cols), F32),
        compiler_params=_cparams(("parallel", "parallel")),
        name="fnet_stage_b",
    )(tr2, ti2, cb, sb)
    return out.reshape(SEQ_T, n_cols)


def _chan_dft_body(x_ref, w_ref, p_ref, q_ref):
    y = jnp.dot(x_ref[...].astype(BF16), w_ref[...], preferred_element_type=F32)
    p_ref[...] = y[:, :256]
    q_ref[...] = y[:, 256:]


def _chan_dft(z, off, chan):
    oshape = jax.ShapeDtypeStruct((SEQ_T, C_BR), F32)
    return pl.pallas_call(
        _chan_dft_body,
        grid=(SEQ_T // ROW_TILE, 4),
        in_specs=[
            pl.BlockSpec((ROW_TILE, 256), lambda i, g: (i, off // 256 + g)),
            pl.BlockSpec((256, 512), lambda i, g: (0, 0)),
        ],
        out_specs=[pl.BlockSpec((ROW_TILE, 256), lambda i, g: (i, g))] * 2,
        out_shape=[oshape, oshape],
        compiler_params=_cparams(("parallel", "parallel")),
        name="fnet_chan_dft",
    )(z, chan)


NA_QROWS = 8
NA_SLAB = 16
NA_SCALE = NA_DH ** -0.5


def _na_main_body(q_ref, k_ref, v_ref, km_ref, vm_ref, bt_ref, mb_ref, g_ref, o_ref):
    rb = pl.program_id(1)
    slab0 = jnp.clip(rb * NA_QROWS - NA_KH // 2, 0, GRID_H - NA_SLAB)
    w2 = 2 * NA_DH
    lane = lax.broadcasted_iota(jnp.int32, (GRID_W, w2), 1)
    row2 = lax.broadcasted_iota(jnp.int32, (w2, w2), 0)
    lane2 = lax.broadcasted_iota(jnp.int32, (w2, w2), 1)
    own_head = (row2 >= NA_DH) == (lane2 >= NA_DH)
    km = km_ref[...].astype(BF16)
    vm = vm_ref[...].astype(BF16)
    mbt = mb_ref[...]
    dn_t = (((1,), (1,)), ((), ()))
    dn_k = (((0,), (0,)), ((), ()))
    for i in range(NA_QROWS):
        r = rb * NA_QROWS + i
        r0 = jnp.clip(r - NA_KH // 2, 0, GRID_H - NA_KH)
        off = pl.multiple_of((r0 - slab0) * GRID_W, GRID_W)
        d0 = r0 - r + (NA_KH - 1)
        q = q_ref[i * GRID_W:(i + 1) * GRID_W, :]
        q2 = jnp.where(own_head, jnp.concatenate([q, q], axis=0), 0.0).astype(BF16)
        ks = k_ref[pl.ds(off, NA_KH * GRID_W), :].astype(BF16)
        vs = v_ref[pl.ds(off, NA_KH * GRID_W), :].astype(BF16)
        st = lax.dot_general(ks, q2, dn_t, preferred_element_type=F32) * NA_SCALE + bt_ref[d0]
        sx = lax.dot_general(km, q2, dn_t, preferred_element_type=F32) * NA_SCALE + mbt
        m = jnp.maximum(jnp.max(st, axis=0, keepdims=True), jnp.max(sx, axis=0, keepdims=True))
        p = jnp.exp(st - m)
        px = jnp.exp(sx - m)
        inv = 1.0 / (jnp.sum(p, axis=0, keepdims=True) + jnp.sum(px, axis=0, keepdims=True))
        o2 = lax.dot_general((p * inv).astype(BF16), vs, dn_k, preferred_element_type=F32)
        o2 = o2 + lax.dot_general((px * inv).astype(BF16), vm, dn_k, preferred_element_type=F32)
        o = jnp.where(lane < NA_DH, o2[:GRID_W], o2[GRID_W:])
        g = g_ref[i * GRID_W:(i + 1) * GRID_W, :]
        o_ref[i * GRID_W:(i + 1) * GRID_W, :] = (o * (g * jax.nn.sigmoid(g))).astype(BF16)


def _na_meta_body(q_ref, k_ref, v_ref, mb_ref, g_ref, o_ref):
    lane = lax.broadcasted_iota(jnp.int32, (N_META, 2 * NA_DH), 1)
    q = q_ref[...]
    km = k_ref[...].astype(BF16)
    vm = v_ref[...].astype(BF16)
    dn_t = (((1,), (1,)), ((), ()))
    outs = []
    for hh in range(2):
        sel = (lane >= hh * NA_DH) & (lane < (hh + 1) * NA_DH)
        qm = jnp.where(sel, q, 0.0).astype(BF16)
        s = lax.dot_general(qm, km, dn_t, preferred_element_type=F32) * NA_SCALE + mb_ref[hh, 0:1, :]
        m = jnp.max(s, axis=-1, keepdims=True)
        p = jnp.exp(s - m)
        den = jnp.sum(p, axis=-1, keepdims=True)
        outs.append(jnp.dot(p.astype(BF16), vm, preferred_element_type=F32) / den)
    o = jnp.where(lane < NA_DH, outs[0], outs[1])
    g = g_ref[...]
    o_ref[...] = (o * (g * jax.nn.sigmoid(g))).astype(BF16)


def _na_bias_table(rpb):
    colmask, dc = _na_index_tables()
    rows = np.arange(NA_KH)[:, None] + np.arange(NA_KH)[None, :]
    t = rpb.astype(F32)[:, rows]
    t = t[:, :, :, dc]
    t = jnp.where(colmask[None, None, None], t, NEG_INF)
    t = t.reshape(NA_HEADS // 2, 2, NA_KH, NA_KH, GRID_W, GRID_W)
    t = jnp.transpose(t, (0, 2, 3, 5, 1, 4))
    return t.reshape(NA_HEADS // 2, NA_KH, NA_KH * GRID_W, 2 * GRID_W)


def _neighborhood_attention(z, rpb, meta_bias):
    bt = _na_bias_table(rpb)
    mb = jnp.broadcast_to(meta_bias.astype(F32).reshape(NA_HEADS // 2, 2, 1, N_META),
                          (NA_HEADS // 2, 2, 8, N_META))
    mbt = jnp.transpose(meta_bias.astype(F32).reshape(NA_HEADS // 2, 2, N_META), (0, 2, 1))
    mbt = jnp.repeat(mbt, GRID_W, axis=2)
    w2 = 2 * NA_DH
    qrows = NA_QROWS * GRID_W
    srows = NA_SLAB * GRID_W

    def slab_start(rb):
        start = N_META + GRID_W * jnp.clip(rb * NA_QROWS - NA_KH // 2, 0, GRID_H - NA_SLAB)
        return pl.multiple_of(start, N_META)

    def col(off):
        return lambda hp, rb: (pl.multiple_of(N_META + rb * qrows, N_META), pl.multiple_of(off + hp * w2, w2))

    def slab(off):
        return lambda hp, rb: (slab_start(rb), pl.multiple_of(off + hp * w2, w2))

    def meta(off):
        return lambda hp, rb: (0, off // w2 + hp)

    y_main = pl.pallas_call(
        _na_main_body,
        grid=(NA_HEADS // 2, GRID_H // NA_QROWS),
        in_specs=[
            pl.BlockSpec((pl.Element(qrows), pl.Element(w2)), col(OFF_Q)),
            pl.BlockSpec((pl.Element(srows), pl.Element(w2)), slab(OFF_K)),
            pl.BlockSpec((pl.Element(srows), pl.Element(w2)), slab(OFF_V)),
            pl.BlockSpec((N_META, w2), meta(OFF_K)),
            pl.BlockSpec((N_META, w2), meta(OFF_V)),
            pl.BlockSpec((None, NA_KH, NA_KH * GRID_W, w2), lambda hp, rb: (hp, 0, 0, 0)),
            pl.BlockSpec((None, N_META, w2), lambda hp, rb: (hp, 0, 0)),
            pl.BlockSpec((pl.Element(qrows), pl.Element(w2)), col(OFF_NA_GATE)),
        ],
        out_specs=pl.BlockSpec((qrows, w2), lambda hp, rb: (rb, hp)),
        out_shape=jax.ShapeDtypeStruct((SEQ, C_BR), BF16),
        compiler_params=_cparams(("parallel", "parallel")),
        name="na_main",
    )(z, z, z, z, z, bt, mbt, z)

    def mcol(off):
        return lambda hp: (0, off // w2 + hp)

    y_meta = pl.pallas_call(
        _na_meta_body,
        grid=(NA_HEADS // 2,),
        in_specs=[
            pl.BlockSpec((N_META, w2), mcol(OFF_Q)),
            pl.BlockSpec((N_META, w2), mcol(OFF_K)),
            pl.BlockSpec((N_META, w2), mcol(OFF_V)),
            pl.BlockSpec((None, 2, 8, N_META), lambda hp: (hp, 0, 0, 0)),
            pl.BlockSpec((N_META, w2), mcol(OFF_NA_GATE)),
        ],
        out_specs=pl.BlockSpec((N_META, w2), lambda hp: (0, hp)),
        out_shape=jax.ShapeDtypeStruct((N_META, C_BR), BF16),
        compiler_params=_cparams(("parallel",)),
        name="na_meta",
    )(z, z, z, mb, z)
    return y_meta, y_main


def _silu(g):
    return g * jax.nn.sigmoid(g)


def _merge_body(ya_ref, yb_ref, yc_ref, hg_ref, fg_ref, wa_ref, wb_ref, wc_ref, ga_ref, gb_ref, gc_ref, o_ref,
                sa_ref, sb_ref):
    @pl.when(pl.program_id(1) == 0)
    def _():
        sa_ref[...] = (ya_ref[...] * _silu(hg_ref[...])).astype(BF16)
        sb_ref[...] = (yb_ref[...] * _silu(fg_ref[...])).astype(BF16)

    dot = lambda a, b: jnp.dot(a[...], b[...], preferred_element_type=F32)
    m = jax.nn.sigmoid(ga_ref[...]) * dot(sa_ref, wa_ref)
    m = m + jax.nn.sigmoid(gb_ref[...]) * dot(sb_ref, wb_ref)
    m = m + jax.nn.sigmoid(gc_ref[...]) * dot(yc_ref, wc_ref)
    o_ref[...] = m.astype(BF16)


def _merge(ya, yb, yc, wa, wb, wc, z, tn=512):
    n_rows = ya.shape[0]
    yspec = pl.BlockSpec((ROW_TILE, C_BR), lambda i, j: (i, 0))
    wspec = pl.BlockSpec((C_BR, tn), lambda i, j: (0, j))
    zspec = lambda off: pl.BlockSpec((ROW_TILE, C_BR), lambda i, j, off=off: (i, off // C_BR))
    gspec = lambda k: pl.BlockSpec((ROW_TILE, tn), lambda i, j, k=k: (i, (OFF_MERGE + k * D_MODEL) // tn + j))
    return pl.pallas_call(
        _merge_body,
        grid=(n_rows // ROW_TILE, D_MODEL // tn),
        in_specs=[yspec, yspec, yspec, zspec(OFF_HY_GATE), zspec(OFF_FN_GATE), wspec, wspec, wspec,
                  gspec(0), gspec(1), gspec(2)],
        out_specs=pl.BlockSpec((ROW_TILE, tn), lambda i, j: (i, j)),
        out_shape=jax.ShapeDtypeStruct((n_rows, D_MODEL), BF16),
        scratch_shapes=[pltpu.VMEM((ROW_TILE, C_BR), BF16), pltpu.VMEM((ROW_TILE, C_BR), BF16)],
        compiler_params=_cparams(("parallel", "arbitrary")),
        name="merge",
    )(ya, yb, yc, z, z, wa, wb, wc, z, z, z)


def _outproj_body(h_ref, m_ref, w_ref, o_ref):
    o_ref[...] = h_ref[...] + jnp.dot(m_ref[...], w_ref[...], preferred_element_type=F32)


def _outproj(h, m, w, tn=1024):
    n_rows = h.shape[0]
    return pl.pallas_call(
        _outproj_body,
        grid=(n_rows // ROW_TILE, D_MODEL // tn),
        in_specs=[
            pl.BlockSpec((ROW_TILE, tn), lambda i, j: (i, j)),
            pl.BlockSpec((ROW_TILE, D_MODEL), lambda i, j: (i, 0)),
            pl.BlockSpec((D_MODEL, tn), lambda i, j: (0, j)),
        ],
        out_specs=pl.BlockSpec((ROW_TILE, tn), lambda i, j: (i, j)),
        out_shape=jax.ShapeDtypeStruct((n_rows, D_MODEL), F32),
        compiler_params=_cparams(("parallel", "parallel")),
        name="outproj",
    )(h, m, w)


def _final_norm_body(x_ref, g_ref, o_ref):
    x = x_ref[...]
    y = x * lax.rsqrt(jnp.mean(x * x, axis=-1, keepdims=True) + EPS)
    o_ref[...] = y * g_ref[...]


def _final_norm(h, g, tm=512):
    return pl.pallas_call(
        _final_norm_body,
        grid=(SEQ // tm,),
        in_specs=[
            pl.BlockSpec((pl.Element(tm), pl.Element(D_MODEL)), lambda i: (pl.multiple_of(N_META + i * tm, N_META), 0)),
            pl.BlockSpec((1, D_MODEL), lambda i: (0, 0)),
        ],
        out_specs=pl.BlockSpec((tm, D_MODEL), lambda i: (i, 0)),
        out_shape=jax.ShapeDtypeStruct((SEQ, D_MODEL), F32),
        compiler_params=_cparams(("parallel",)),
        name="final_norm",
    )(h, g.reshape(1, D_MODEL))


def _hyena_branch(z, conv_w, conv_b, w1, b1, w2, b2, w3, b3, w4, freq, decay, skip):
    tab = _hyena_tables()
    fwd = [[_bf(tab["fwd_r"])], [_bf(tab["fwd_i"])]]
    inv_r, inv_i = _bf(tab["inv_r"]), _bf(tab["inv_i"])
    cb, sb = _bf(tab["cb"]), _bf(tab["sb"])

    hyc = _shortconv(z, conv_w, conv_b)
    filt = _hyena_filters(w1, b1, w2, b2, w3, b3, w4, freq, decay)
    ftr, fti = _stage_a([filt], [0], filt.shape[-1], fwd, BF16)
    kr, ki = _filter_spectrum(ftr, fti, cb, sb)

    hyc = _view3(hyc)
    tr, ti = _stage_a([hyc], [0], C_BR, fwd, BF16)
    tr, ti = _conv_stage_b(tr, ti, kr, ki, 0, cb, sb)
    zmid = _stage_ainv(tr, ti, inv_r, inv_i, hyc, 0, hyc, C_BR, skip[0], flat=False)

    tr, ti = _stage_a([zmid], [0], C_BR, fwd, BF16)
    tr, ti = _conv_stage_b(tr, ti, kr, ki, 1, cb, sb)
    return _stage_ainv(tr, ti, inv_r, inv_i, zmid, 0, hyc, 2 * C_BR, skip[1], flat=True)


def _fnet_branch(z):
    tab = _fnet_tables()
    cs, sn = _bf(tab["cs"]), _bf(tab["sn"])
    p, q = _chan_dft(z, OFF_FN_IN, _bf(tab["chan"]))
    tr, ti = _stage_a([p, q], [0, 0], C_BR, [[cs, -sn], [-sn, -cs]], BF16)
    return _fnet_stage_b(tr, ti, _bf(tab["cb"]), _bf(tab["sb"]))


def _layer(h, norm_g, w_in, conv_w, conv_b, w1, b1, w2, b2, w3, b3, w4, freq, decay, skip, rpb, meta_bias,
           w_a, w_b, w_c, w_out):
    z = _inproj(h, norm_g, _to_bf16(w_in))
    ya = _hyena_branch(z, conv_w, conv_b, w1, b1, w2, b2, w3, b3, w4, freq, decay, skip)
    yb = _fnet_branch(z)
    yc_meta, yc_main = _neighborhood_attention(z, rpb, meta_bias)
    yc = jnp.concatenate([yc_meta, yc_main], axis=0)
    m = _merge(ya, yb, yc, w_a.astype(BF16), w_b.astype(BF16), w_c.astype(BF16), z)
    return _outproj(h, m, w_out.astype(BF16))


def kernel(x, meta_tokens, norm_g, w_in, hy_conv_w, hy_conv_b, hy_flt_w1, hy_flt_b1, hy_flt_w2, hy_flt_b2,
           hy_flt_w3, hy_flt_b3, hy_flt_w4, hy_flt_freq, hy_decay, hy_skip, na_rpb, na_meta_bias,
           w_branch_a, w_branch_b, w_branch_c, w_out, final_g):
    assert x.shape == (1, SEQ, D_MODEL)
    h = jnp.concatenate([meta_tokens.astype(x.dtype), x[0]], axis=0)
    for i in range(norm_g.shape[0]):
        h = _layer(h, norm_g[i], w_in[i], hy_conv_w[i], hy_conv_b[i], hy_flt_w1[i], hy_flt_b1[i],
                   hy_flt_w2[i], hy_flt_b2[i], hy_flt_w3[i], hy_flt_b3[i], hy_flt_w4[i], hy_flt_freq[i],
                   hy_decay[i].reshape(-1), hy_skip[i], na_rpb[i], na_meta_bias[i],
                   w_branch_a[i], w_branch_b[i], w_branch_c[i], w_out[i])
    return _final_norm(h, final_g)[None]
```

```python
import functools
import math

import numpy as np
import jax
import jax.numpy as jnp
from jax import lax
from jax.experimental import pallas as pl
from jax.experimental.pallas import tpu as pltpu

F32 = jnp.float32
BF16 = jnp.bfloat16

D_MODEL = 2048
SEQ = 16384
N_META = 16
SEQ_T = SEQ + N_META
GRID_W = 64
GRID_H = SEQ // GRID_W
C_BR = 1024
HY_EMB = 33
HY_HID = 64
NA_HEADS = 16
NA_DH = 64
NA_KH = 8
NA_KW = 16
EPS = 1e-6
NEG_INF = -1e30

OFF_HY_IN = 0
OFF_HY_GATE = 3072
OFF_FN_IN = 4096
OFF_FN_GATE = 5120
OFF_Q = 6144
OFF_K = 7168
OFF_V = 8192
OFF_NA_GATE = 9216
OFF_MERGE = 10240
N_IN = 16384

FA = 80
FB = 205
FBP = 208
BBLK = 8
KA_HY = FA + 1
KAP_HY = 96

ROW_TILE = 656
LANE = 128
VMEM_LIMIT = 48 * 1024 * 1024


def _cparams(sem):
    return pltpu.CompilerParams(dimension_semantics=sem, vmem_limit_bytes=VMEM_LIMIT)


@functools.lru_cache(maxsize=None)
def _hyena_tables():
    n_circ = 2 * SEQ_T
    a = np.arange(FA)[None, None, :]
    b = np.arange(FBP)[:, None, None]
    ka = np.arange(KAP_HY)[None, :, None]
    n = FB * a + b
    ang = 2.0 * np.pi * ((ka * n) % n_circ) / n_circ
    valid = (b < FB) & (ka < KA_HY)
    fwd_r = np.where(valid, np.cos(ang), 0.0)
    fwd_i = np.where(valid, -np.sin(ang), 0.0)
    c = np.where((ka == 0) | (ka == FA), 1.0, 2.0) / n_circ
    inv_r = np.transpose(np.where(valid, c * np.cos(ang), 0.0), (0, 2, 1))
    inv_i = np.transpose(np.where(valid, -c * np.sin(ang), 0.0), (0, 2, 1))
    kb = np.arange(FBP)[:, None]
    bb = np.arange(FBP)[None, :]
    phi = 2.0 * np.pi * ((kb * bb) % FB) / FB
    ok = (kb < FB) & (bb < FB)
    cb = np.where(ok, np.cos(phi), 0.0)
    sb = np.where(ok, np.sin(phi), 0.0)
    return dict(fwd_r=fwd_r, fwd_i=fwd_i, inv_r=inv_r, inv_i=inv_i, cb=cb, sb=sb)


@functools.lru_cache(maxsize=None)
def _fnet_tables():
    a = np.arange(FA)[None, None, :]
    b = np.arange(FBP)[:, None, None]
    ka = np.arange(FA)[None, :, None]
    n = FB * a + b
    ang = 2.0 * np.pi * ((ka * n) % SEQ_T) / SEQ_T
    valid = np.broadcast_to(b < FB, ang.shape)
    cs = np.where(valid, np.cos(ang), 0.0)
    sn = np.where(valid, np.sin(ang), 0.0)
    kb = np.arange(FBP)[:, None]
    bb = np.arange(FBP)[None, :]
    phi = 2.0 * np.pi * ((kb * bb) % FB) / FB
    ok = (kb < FB) & (bb < FB)
    scale = 1.0 / math.sqrt(SEQ_T * 256.0)
    cb = np.where(ok, np.cos(phi), 0.0) * scale
    sb = np.where(ok, np.sin(phi), 0.0) * scale
    j = np.arange(256)
    th = 2.0 * np.pi * ((j[:, None] * j[None, :]) % 256) / 256.0
    chan = np.concatenate([np.cos(th), np.sin(th)], axis=1)
    return dict(cs=cs, sn=sn, cb=cb, sb=sb, chan=chan)


@functools.lru_cache(maxsize=None)
def _filter_features():
    t = np.linspace(0.0, 1.0, SEQ_T)[:, None]
    bands = (HY_EMB - 1) // 2
    w = 2.0 * np.pi * np.arange(SEQ_T)[:, None] / SEQ_T
    f = np.linspace(1e-4, bands - 1, bands)[None, :]
    z = np.concatenate([t, np.cos(f * w), -np.sin(f * w)], axis=-1)
    out = np.zeros((SEQ_T, HY_HID), np.float64)
    out[:, :HY_EMB] = z
    return out


@functools.lru_cache(maxsize=None)
def _na_index_tables():
    qc = np.arange(GRID_W)[:, None]
    kc = np.arange(GRID_W)[None, :]
    cs = np.clip(qc - NA_KW // 2, 0, GRID_W - NA_KW)
    colmask = (kc >= cs) & (kc < cs + NA_KW)
    dc = np.clip(kc - qc + NA_KW - 1, 0, 2 * NA_KW - 2)
    return colmask, dc


def _bf(x):
    return jnp.asarray(np.asarray(x, np.float32), dtype=BF16)


def _inproj_body(x_ref, g_ref, w_ref, o_ref, xn_ref):
    @pl.when(pl.program_id(1) == 0)
    def _():
        x = x_ref[...]
        y = x * lax.rsqrt(jnp.mean(x * x, axis=-1, keepdims=True) + EPS)
        xn_ref[...] = (y * g_ref[...]).astype(BF16)

    o_ref[...] = jnp.dot(xn_ref[...], w_ref[...], preferred_element_type=F32)


def _cast_body(x_ref, o_ref):
    o_ref[...] = x_ref[...].astype(o_ref.dtype)


def _to_bf16(w_stack, layer, tm=128):
    _, n_rows, n_cols = w_stack.shape
    return pl.pallas_call(
        _cast_body,
        grid=(n_rows // tm,),
        in_specs=[pl.BlockSpec((None, tm, n_cols), lambda i: (layer, i, 0))],
        out_specs=pl.BlockSpec((tm, n_cols), lambda i: (i, 0)),
        out_shape=jax.ShapeDtypeStruct((n_rows, n_cols), BF16),
        compiler_params=_cparams(("parallel",)),
        name="cast_bf16",
    )(w_stack)


def _inproj(h, g, w_bf16, tn=2048):
    n_rows, d = h.shape
    n_out = w_bf16.shape[1]
    return pl.pallas_call(
        _inproj_body,
        grid=(n_rows // ROW_TILE, n_out // tn),
        in_specs=[
            pl.BlockSpec((ROW_TILE, d), lambda i, j: (i, 0)),
            pl.BlockSpec((1, d), lambda i, j: (0, 0)),
            pl.BlockSpec((d, tn), lambda i, j: (0, j)),
        ],
        out_specs=pl.BlockSpec((ROW_TILE, tn), lambda i, j: (i, j)),
        out_shape=jax.ShapeDtypeStruct((n_rows, n_out), F32),
        scratch_shapes=[pltpu.VMEM((ROW_TILE, d), BF16)],
        compiler_params=_cparams(("parallel", "arbitrary")),
        name="inproj",
    )(h, g.reshape(1, d), w_bf16)


def _shortconv_body(prev_ref, cur_ref, next_ref, w_ref, b_ref, o_ref):
    i = pl.program_id(0)
    last = pl.num_programs(0) - 1
    x = cur_ref[...]
    rows = lax.broadcasted_iota(jnp.int32, x.shape, 0)
    prev_row = jnp.where(i == 0, 0.0, prev_ref[7:8, :])
    next_row = jnp.where(i == last, 0.0, next_ref[0:1, :])
    up = jnp.where(rows == 0, prev_row, pltpu.roll(x, 1, 0))
    dn = jnp.where(rows == ROW_TILE - 1, next_row, pltpu.roll(x, ROW_TILE - 1, 0))
    y = up * w_ref[0:1, :]
    y = y + x * w_ref[1:2, :]
    y = y + dn * w_ref[2:3, :]
    o_ref[...] = y + b_ref[...]


def _shortconv(z, w, b, tc=1024):
    n_rows = z.shape[0]
    n_c = w.shape[1]
    n8 = n_rows // 8
    r8 = ROW_TILE // 8
    return pl.pallas_call(
        _shortconv_body,
        grid=(n_rows // ROW_TILE, n_c // tc),
        in_specs=[
            pl.BlockSpec((8, tc), lambda i, j: (jnp.maximum(i * r8 - 1, 0), j)),
            pl.BlockSpec((ROW_TILE, tc), lambda i, j: (i, j)),
            pl.BlockSpec((8, tc), lambda i, j: (jnp.minimum((i + 1) * r8, n8 - 1), j)),
            pl.BlockSpec((3, tc), lambda i, j: (0, j)),
            pl.BlockSpec((1, tc), lambda i, j: (0, j)),
        ],
        out_specs=pl.BlockSpec((ROW_TILE, tc), lambda i, j: (i, j)),
        out_shape=jax.ShapeDtypeStruct((n_rows, n_c), F32),
        compiler_params=_cparams(("parallel", "parallel")),
        name="shortconv",
    )(z, z, z, w, b.reshape(1, n_c))


def _hp_dot(a, b):
    return jnp.dot(a, b, preferred_element_type=F32, precision=lax.Precision.HIGHEST)


def _split_bf16(x):
    hi = x.astype(BF16)
    return hi, (x - hi.astype(F32)).astype(BF16)


def _filter_body(z_ref, w1_ref, b1_ref, w2_ref, b2_ref, w3_ref, b3_ref, fr_ref, fr3_ref, w4_ref, dec_ref, o_ref,
                 h_ref):
    i = pl.program_id(0)
    j = pl.program_id(1)

    @pl.when(j == 0)
    def _():
        fr = fr_ref[...]
        h = jnp.sin(fr * (_hp_dot(z_ref[...], w1_ref[...]) + b1_ref[...]))
        h = jnp.sin(fr * (_hp_dot(h, w2_ref[...]) + b2_ref[...]))
        h3 = jnp.sin(fr3_ref[...] * (_hp_dot(h, w3_ref[...]) + b3_ref[...]))
        hi, lo = _split_bf16(h3)
        lane = lax.broadcasted_iota(jnp.int32, h3.shape, 1)
        h_ref[...] = jnp.where((lane >= HY_HID) & (lane < 2 * HY_HID), lo, hi)

    t = z_ref[:, 0:1]
    whi, wlo = _split_bf16(w4_ref[...])
    wcat = jnp.concatenate([whi, whi, wlo], axis=0)
    y = jnp.dot(h_ref[...], wcat, preferred_element_type=F32) * jnp.exp(-t * jnp.abs(dec_ref[...]))
    rows = lax.broadcasted_iota(jnp.int32, y.shape, 0)
    drop = jnp.logical_and(jnp.logical_and(i == 0, j % 2 == 1), rows == 0)
    o_ref[...] = jnp.where(drop, 0.0, y)


def _hyena_filters(w1, b1, w2, b2, w3, b3, w4, freq, decay):
    zfeat = jnp.asarray(_filter_features(), dtype=F32)
    w1p = jnp.zeros((HY_HID, HY_HID), F32).at[:HY_EMB].set(w1)
    n_c = w4.shape[1]
    tc = C_BR
    small = lambda i, j: (0, 0)
    return pl.pallas_call(
        _filter_body,
        grid=(SEQ_T // ROW_TILE, n_c // tc),
        in_specs=[
            pl.BlockSpec((ROW_TILE, HY_HID), lambda i, j: (i, 0)),
            pl.BlockSpec((HY_HID, HY_HID), small),
            pl.BlockSpec((1, HY_HID), small),
            pl.BlockSpec((HY_HID, HY_HID), small),
            pl.BlockSpec((1, HY_HID), small),
            pl.BlockSpec((HY_HID, 3 * HY_HID), small),
            pl.BlockSpec((1, 3 * HY_HID), small),
            pl.BlockSpec((1, HY_HID), small),
            pl.BlockSpec((1, 3 * HY_HID), small),
            pl.BlockSpec((HY_HID, tc), lambda i, j: (0, j)),
            pl.BlockSpec((1, tc), lambda i, j: (0, j)),
        ],
        out_specs=pl.BlockSpec((ROW_TILE, tc), lambda i, j: (i, j)),
        out_shape=jax.ShapeDtypeStruct((SEQ_T, n_c), F32),
        scratch_shapes=[pltpu.VMEM((ROW_TILE, 3 * HY_HID), BF16)],
        compiler_params=_cparams(("parallel", "arbitrary")),
        name="hyena_filter",
    )(zfeat, w1p, b1.reshape(1, -1), w2, b2.reshape(1, -1), jnp.tile(w3, (1, 3)), jnp.tile(b3.reshape(1, -1), (1, 3)),
      freq.reshape(1, -1), jnp.tile(freq.reshape(1, -1), (1, 3)), w4, decay.reshape(1, n_c))


def _view3(x):
    return x if x.ndim == 3 else x.reshape(FA, FB, x.shape[1])


def _stage_a_body(n_in, n_out, *refs):
    x_refs = refs[:n_in]
    w_refs = refs[n_in:n_in + n_in * n_out]
    o_refs = refs[n_in + n_in * n_out:]
    bb = pl.program_id(0)
    for j in range(BBLK):
        valid = bb * BBLK + j < FB
        xs = [jnp.where(valid, xr[:, j, :], 0.0).astype(BF16) for xr in x_refs]
        for o in range(n_out):
            acc = None
            for k in range(n_in):
                d = jnp.dot(w_refs[o * n_in + k][j], xs[k], preferred_element_type=F32)
                acc = d if acc is None else acc + d
            o_refs[o][j] = acc.astype(o_refs[o].dtype)


def _stage_a(xs, col_offs, n_cols, w_tabs, out_dtype, ct=1024):
    n_in = len(xs)
    n_out = len(w_tabs)
    m = w_tabs[0][0].shape[1]
    x3 = [_view3(x) for x in xs]
    in_specs = []
    for k in range(n_in):
        off = col_offs[k] // ct
        in_specs.append(pl.BlockSpec((FA, BBLK, ct), lambda b, c, off=off: (0, b, off + c)))
    flat_w = []
    for o in range(n_out):
        for k in range(n_in):
            flat_w.append(w_tabs[o][k])
            in_specs.append(pl.BlockSpec((BBLK, m, FA), lambda b, c: (b, 0, 0)))
    return pl.pallas_call(
        functools.partial(_stage_a_body, n_in, n_out),
        grid=(FBP // BBLK, n_cols // ct),
        in_specs=in_specs,
        out_specs=[pl.BlockSpec((BBLK, m, ct), lambda b, c: (b, 0, c)) for _ in range(n_out)],
        out_shape=[jax.ShapeDtypeStruct((FBP, m, n_cols), out_dtype) for _ in range(n_out)],
        compiler_params=_cparams(("parallel", "parallel")),
        name="dft_stage_a",
    )(*x3, *flat_w)


def _stage_ainv_body(tr_ref, ti_ref, wr_ref, wi_ref, v_ref, x_ref, skip_ref, o_ref):
    skip = skip_ref[...]
    for j in range(BBLK):
        y = jnp.dot(wr_ref[j], tr_ref[j].astype(BF16), preferred_element_type=F32)
        y = y + jnp.dot(wi_ref[j], ti_ref[j].astype(BF16), preferred_element_type=F32)
        y = y + v_ref[:, j, :] * skip
        o_ref[:, j, :] = x_ref[:, j, :] * y


def _stage_ainv(tr, ti, wr, wi, v, v_off, xmul, x_off, skip, flat, ct=1024):
    n_cols = tr.shape[2]
    kap = tr.shape[1]
    view = _view3
    spec3 = lambda off: pl.BlockSpec((FA, BBLK, ct), lambda b, c, off=off // ct: (0, b, off + c))
    in_specs = [
        pl.BlockSpec((BBLK, kap, ct), lambda b, c: (b, 0, c)),
        pl.BlockSpec((BBLK, kap, ct), lambda b, c: (b, 0, c)),
        pl.BlockSpec((BBLK, FA, kap), lambda b, c: (b, 0, 0)),
        pl.BlockSpec((BBLK, FA, kap), lambda b, c: (b, 0, 0)),
        spec3(v_off),
        spec3(x_off),
        pl.BlockSpec((1, ct), lambda b, c: (0, c)),
    ]
    args = [tr, ti, wr, wi, view(v), view(xmul), skip.reshape(1, n_cols)]
    out = pl.pallas_call(
        _stage_ainv_body,
        grid=(FBP // BBLK, n_cols // ct),
        in_specs=in_specs,
        out_specs=pl.BlockSpec((FA, BBLK, ct), lambda b, c: (0, b, c)),
        out_shape=jax.ShapeDtypeStruct((FA, FB, n_cols), F32),
        compiler_params=_cparams(("parallel", "parallel")),
        name="dft_stage_a_inv",
    )(*args)
    return out.reshape(SEQ_T, n_cols) if flat else out


def _filter_spec_body(fr_ref, fi_ref, br_ref, bi_ref, c_ref, s_ref, kr_ref, ki_ref):
    c = c_ref[...]
    s = s_ref[...]
    dot = lambda a, b: jnp.dot(a, b[...], preferred_element_type=F32)
    kr_ref[...] = dot(c, fr_ref) + dot(s, fi_ref) + dot(c, br_ref) + dot(s, bi_ref)
    ki_ref[...] = dot(c, fi_ref) - dot(s, fr_ref) - dot(c, bi_ref) + dot(s, br_ref)


def _filter_spectrum(tr, ti, cb, sb, ct=1024):
    nc = C_BR // ct
    kap = tr.shape[1]
    n_cols = tr.shape[2]
    tr2 = tr.reshape(FBP, kap * n_cols)
    ti2 = ti.reshape(FBP, kap * n_cols)
    ncol_blk = n_cols // ct

    def tspec(direction):
        return pl.BlockSpec((FBP, ct), lambda o, ka, c: (0, ka * ncol_blk + (2 * o + direction) * nc + c))

    mat = pl.BlockSpec((FBP, FBP), lambda o, ka, c: (0, 0))
    ospec = pl.BlockSpec((None, None, FBP, ct), lambda o, ka, c: (o, ka, 0, c))
    oshape = jax.ShapeDtypeStruct((2, KA_HY, FBP, C_BR), F32)
    return pl.pallas_call(
        _filter_spec_body,
        grid=(2, KA_HY, nc),
        in_specs=[tspec(0), tspec(0), tspec(1), tspec(1), mat, mat],
        out_specs=[ospec, ospec],
        out_shape=[oshape, oshape],
        compiler_params=_cparams(("parallel", "parallel", "parallel")),
        name="filter_spectrum",
    )(tr2, ti2, tr2, ti2, cb, sb)


def _conv_b_body(tr_ref, ti_ref, kr_ref, ki_ref, c_ref, s_ref, or_ref, oi_ref):
    ka = pl.program_id(0)

    @pl.when(ka < KA_HY)
    def _():
        c = c_ref[...]
        s = s_ref[...]
        dot = lambda a, b: jnp.dot(a, b, preferred_element_type=F32)
        tr = tr_ref[...]
        ti = ti_ref[...]
        yr = dot(c, tr) + dot(s, ti)
        yi = dot(c, ti) - dot(s, tr)
        kr = kr_ref[...]
        ki = ki_ref[...]
        zr = (yr * kr - yi * ki).astype(BF16)
        zi = (yr * ki + yi * kr).astype(BF16)
        or_ref[...] = (dot(c, zr) - dot(s, zi)).astype(BF16)
        oi_ref[...] = (dot(c, zi) + dot(s, zr)).astype(BF16)

    @pl.when(ka >= KA_HY)
    def _():
        or_ref[...] = jnp.zeros_like(or_ref)
        oi_ref[...] = jnp.zeros_like(oi_ref)


def _conv_stage_b(tr, ti, kr, ki, order, cb, sb, ct=1024):
    kap = tr.shape[1]
    n_cols = tr.shape[2]
    nc = n_cols // ct
    tr2 = tr.reshape(FBP, kap * n_cols)
    ti2 = ti.reshape(FBP, kap * n_cols)
    tspec = pl.BlockSpec((FBP, ct), lambda ka, c: (0, ka * nc + c))
    kspec = pl.BlockSpec((None, None, FBP, ct), lambda ka, c: (order, jnp.minimum(ka, KA_HY - 1), 0, c))
    mat = pl.BlockSpec((FBP, FBP), lambda ka, c: (0, 0))
    oshape = jax.ShapeDtypeStruct((FBP, kap * n_cols), BF16)
    o_r, o_i = pl.pallas_call(
        _conv_b_body,
        grid=(kap, nc),
        in_specs=[tspec, tspec, kspec, kspec, mat, mat],
        out_specs=[tspec, tspec],
        out_shape=[oshape, oshape],
        compiler_params=_cparams(("parallel", "parallel")),
        name="conv_stage_b",
    )(tr2, ti2, kr, ki, cb, sb)
    return o_r.reshape(FBP, kap, n_cols), o_i.reshape(FBP, kap, n_cols)


def _fnet_b_body(tr_ref, ti_ref, c_ref, s_ref, o_ref):
    dot = lambda a, b: jnp.dot(a, b, preferred_element_type=F32)
    y = dot(c_ref[...], tr_ref[...]) + dot(s_ref[...], ti_ref[...])
    o_ref[...] = y[:FB, :]


def _fnet_stage_b(tr, ti, cb, sb, ct=1024):
    n_cols = tr.shape[2]
    nc = n_cols // ct
    tr2 = tr.reshape(FBP, FA * n_cols)
    ti2 = ti.reshape(FBP, FA * n_cols)
    tspec = pl.BlockSpec((FBP, ct), lambda ka, c: (0, ka * nc + c))
    mat = pl.BlockSpec((FBP, FBP), lambda ka, c: (0, 0))
    out = pl.pallas_call(
        _fnet_b_body,
        grid=(FA, nc),
        in_specs=[tspec, tspec, mat, mat],
        out_specs=pl.BlockSpec((FB, ct), lambda ka, c: (0, ka * nc + c)),
        out_shape=jax.ShapeDtypeStruct((FB, FA * n_cols), F32),
        compiler_params=_cparams(("parallel", "parallel")),
        name="fnet_stage_b",
    )(tr2, ti2, cb, sb)
    return out.reshape(SEQ_T, n_cols)


def _chan_dft_body(x_ref, w_ref, p_ref, q_ref):
    y = jnp.dot(x_ref[...].astype(BF16), w_ref[...], preferred_element_type=F32)
    p_ref[...] = y[:, :256]
    q_ref[...] = y[:, 256:]


def _chan_dft(z, off, chan):
    oshape = jax.ShapeDtypeStruct((SEQ_T, C_BR), F32)
    return pl.pallas_call(
        _chan_dft_body,
        grid=(SEQ_T // ROW_TILE, 4),
        in_specs=[
            pl.BlockSpec((ROW_TILE, 256), lambda i, g: (i, off // 256 + g)),
            pl.BlockSpec((256, 512), lambda i, g: (0, 0)),
        ],
        out_specs=[pl.BlockSpec((ROW_TILE, 256), lambda i, g: (i, g))] * 2,
        out_shape=[oshape, oshape],
        compiler_params=_cparams(("parallel", "parallel")),
        name="fnet_chan_dft",
    )(z, chan)


NA_QROWS = 8
NA_SLAB = 16
NA_SCALE = NA_DH ** -0.5


def _na_main_body(q_ref, k_ref, v_ref, km_ref, vm_ref, bt_ref, mb_ref, g_ref, o_ref):
    rb = pl.program_id(1)
    slab0 = jnp.clip(rb * NA_QROWS - NA_KH // 2, 0, GRID_H - NA_SLAB)
    w2 = 2 * NA_DH
    lane = lax.broadcasted_iota(jnp.int32, (GRID_W, w2), 1)
    row2 = lax.broadcasted_iota(jnp.int32, (w2, w2), 0)
    lane2 = lax.broadcasted_iota(jnp.int32, (w2, w2), 1)
    own_head = (row2 >= NA_DH) == (lane2 >= NA_DH)
    km = km_ref[...].astype(BF16)
    vm = vm_ref[...].astype(BF16)
    mbt = mb_ref[...]
    dn_t = (((1,), (1,)), ((), ()))
    dn_k = (((0,), (0,)), ((), ()))
    for i in range(NA_QROWS):
        r = rb * NA_QROWS + i
        r0 = jnp.clip(r - NA_KH // 2, 0, GRID_H - NA_KH)
        off = pl.multiple_of((r0 - slab0) * GRID_W, GRID_W)
        d0 = r0 - r + (NA_KH - 1)
        q = q_ref[i * GRID_W:(i + 1) * GRID_W, :]
        q2 = jnp.where(own_head, jnp.concatenate([q, q], axis=0), 0.0).astype(BF16)
        ks = k_ref[pl.ds(off, NA_KH * GRID_W), :].astype(BF16)
        vs = v_ref[pl.ds(off, NA_KH * GRID_W), :].astype(BF16)
        st = lax.dot_general(ks, q2, dn_t, preferred_element_type=F32) * NA_SCALE + bt_ref[d0]
        sx = lax.dot_general(km, q2, dn_t, preferred_element_type=F32) * NA_SCALE + mbt
        m = jnp.maximum(jnp.max(st, axis=0, keepdims=True), jnp.max(sx, axis=0, keepdims=True))
        p = jnp.exp(st - m)
        px = jnp.exp(sx - m)
        inv = 1.0 / (jnp.sum(p, axis=0, keepdims=True) + jnp.sum(px, axis=0, keepdims=True))
        o2 = lax.dot_general((p * inv).astype(BF16), vs, dn_k, preferred_element_type=F32)
        o2 = o2 + lax.dot_general((px * inv).astype(BF16), vm, dn_k, preferred_element_type=F32)
        o = jnp.where(lane < NA_DH, o2[:GRID_W], o2[GRID_W:])
        g = g_ref[i * GRID_W:(i + 1) * GRID_W, :]
        o_ref[i * GRID_W:(i + 1) * GRID_W, :] = (o * (g * jax.nn.sigmoid(g))).astype(BF16)


def _na_meta_body(q_ref, k_ref, v_ref, mb_ref, g_ref, y_ref, o_ref):
    del y_ref
    lane = lax.broadcasted_iota(jnp.int32, (N_META, 2 * NA_DH), 1)
    q = q_ref[...]
    km = k_ref[...].astype(BF16)
    vm = v_ref[...].astype(BF16)
    dn_t = (((1,), (1,)), ((), ()))
    outs = []
    for hh in range(2):
        sel = (lane >= hh * NA_DH) & (lane < (hh + 1) * NA_DH)
        qm = jnp.where(sel, q, 0.0).astype(BF16)
        s = lax.dot_general(qm, km, dn_t, preferred_element_type=F32) * NA_SCALE + mb_ref[hh, 0:1, :]
        m = jnp.max(s, axis=-1, keepdims=True)
        p = jnp.exp(s - m)
        den = jnp.sum(p, axis=-1, keepdims=True)
        outs.append(jnp.dot(p.astype(BF16), vm, preferred_element_type=F32) / den)
    o = jnp.where(lane < NA_DH, outs[0], outs[1])
    g = g_ref[...]
    o_ref[...] = (o * (g * jax.nn.sigmoid(g))).astype(BF16)


def _na_bias_table(rpb):
    colmask, dc = _na_index_tables()
    rows = np.arange(NA_KH)[:, None] + np.arange(NA_KH)[None, :]
    t = rpb.astype(F32)[:, rows]
    t = t[:, :, :, dc]
    t = jnp.where(colmask[None, None, None], t, NEG_INF)
    t = t.reshape(NA_HEADS // 2, 2, NA_KH, NA_KH, GRID_W, GRID_W)
    t = jnp.transpose(t, (0, 2, 3, 5, 1, 4))
    return t.reshape(NA_HEADS // 2, NA_KH, NA_KH * GRID_W, 2 * GRID_W)


def _neighborhood_attention(z, rpb, meta_bias):
    bt = _na_bias_table(rpb)
    mb = jnp.broadcast_to(meta_bias.astype(F32).reshape(NA_HEADS // 2, 2, 1, N_META),
                          (NA_HEADS // 2, 2, 8, N_META))
    mbt = jnp.transpose(meta_bias.astype(F32).reshape(NA_HEADS // 2, 2, N_META), (0, 2, 1))
    mbt = jnp.repeat(mbt, GRID_W, axis=2)
    w2 = 2 * NA_DH
    qrows = NA_QROWS * GRID_W
    srows = NA_SLAB * GRID_W

    def slab_start(rb):
        start = N_META + GRID_W * jnp.clip(rb * NA_QROWS - NA_KH // 2, 0, GRID_H - NA_SLAB)
        return pl.multiple_of(start, N_META)

    def col(off):
        return lambda hp, rb: (pl.multiple_of(N_META + rb * qrows, N_META), pl.multiple_of(off + hp * w2, w2))

    def slab(off):
        return lambda hp, rb: (slab_start(rb), pl.multiple_of(off + hp * w2, w2))

    def meta(off):
        return lambda hp, rb: (0, off // w2 + hp)

    y_main = pl.pallas_call(
        _na_main_body,
        grid=(NA_HEADS // 2, GRID_H // NA_QROWS),
        in_specs=[
            pl.BlockSpec((pl.Element(qrows), pl.Element(w2)), col(OFF_Q)),
            pl.BlockSpec((pl.Element(srows), pl.Element(w2)), slab(OFF_K)),
            pl.BlockSpec((pl.Element(srows), pl.Element(w2)), slab(OFF_V)),
            pl.BlockSpec((N_META, w2), meta(OFF_K)),
            pl.BlockSpec((N_META, w2), meta(OFF_V)),
            pl.BlockSpec((None, NA_KH, NA_KH * GRID_W, w2), lambda hp, rb: (hp, 0, 0, 0)),
            pl.BlockSpec((None, N_META, w2), lambda hp, rb: (hp, 0, 0)),
            pl.BlockSpec((pl.Element(qrows), pl.Element(w2)), col(OFF_NA_GATE)),
        ],
        out_specs=pl.BlockSpec((pl.Element(qrows), pl.Element(w2)),
                               lambda hp, rb: (pl.multiple_of(N_META + rb * qrows, N_META),
                                               pl.multiple_of(hp * w2, w2))),
        out_shape=jax.ShapeDtypeStruct((SEQ_T, C_BR), BF16),
        compiler_params=_cparams(("parallel", "parallel")),
        name="na_main",
    )(z, z, z, z, z, bt, mbt, z)

    def mcol(off):
        return lambda hp: (0, off // w2 + hp)

    return pl.pallas_call(
        _na_meta_body,
        grid=(NA_HEADS // 2,),
        in_specs=[
            pl.BlockSpec((N_META, w2), mcol(OFF_Q)),
            pl.BlockSpec((N_META, w2), mcol(OFF_K)),
            pl.BlockSpec((N_META, w2), mcol(OFF_V)),
            pl.BlockSpec((None, 2, 8, N_META), lambda hp: (hp, 0, 0, 0)),
            pl.BlockSpec((N_META, w2), mcol(OFF_NA_GATE)),
            pl.BlockSpec((N_META, w2), lambda hp: (0, hp)),
        ],
        out_specs=pl.BlockSpec((N_META, w2), lambda hp: (0, hp)),
        out_shape=jax.ShapeDtypeStruct((SEQ_T, C_BR), BF16),
        input_output_aliases={5: 0},
        compiler_params=_cparams(("parallel",)),
        name="na_meta",
    )(z, z, z, mb, z, y_main)


def _silu(g):
    return g * jax.nn.sigmoid(g)


def _merge_body(ya_ref, yb_ref, yc_ref, hg_ref, fg_ref, wa_ref, wb_ref, wc_ref, ga_ref, gb_ref, gc_ref, o_ref,
                sa_ref, sb_ref):
    @pl.when(pl.program_id(1) == 0)
    def _():
        sa_ref[...] = (ya_ref[...] * _silu(hg_ref[...])).astype(BF16)
        sb_ref[...] = (yb_ref[...] * _silu(fg_ref[...])).astype(BF16)

    dot = lambda a, b: jnp.dot(a[...], b[...], preferred_element_type=F32)
    m = jax.nn.sigmoid(ga_ref[...]) * dot(sa_ref, wa_ref)
    m = m + jax.nn.sigmoid(gb_ref[...]) * dot(sb_ref, wb_ref)
    m = m + jax.nn.sigmoid(gc_ref[...]) * dot(yc_ref, wc_ref)
    o_ref[...] = m.astype(BF16)


def _merge(ya, yb, yc, wa, wb, wc, z, tn=512):
    n_rows = ya.shape[0]
    yspec = pl.BlockSpec((ROW_TILE, C_BR), lambda i, j: (i, 0))
    wspec = pl.BlockSpec((C_BR, tn), lambda i, j: (0, j))
    zspec = lambda off: pl.BlockSpec((ROW_TILE, C_BR), lambda i, j, off=off: (i, off // C_BR))
    gspec = lambda k: pl.BlockSpec((ROW_TILE, tn), lambda i, j, k=k: (i, (OFF_MERGE + k * D_MODEL) // tn + j))
    return pl.pallas_call(
        _merge_body,
        grid=(n_rows // ROW_TILE, D_MODEL // tn),
        in_specs=[yspec, yspec, yspec, zspec(OFF_HY_GATE), zspec(OFF_FN_GATE), wspec, wspec, wspec,
                  gspec(0), gspec(1), gspec(2)],
        out_specs=pl.BlockSpec((ROW_TILE, tn), lambda i, j: (i, j)),
        out_shape=jax.ShapeDtypeStruct((n_rows, D_MODEL), BF16),
        scratch_shapes=[pltpu.VMEM((ROW_TILE, C_BR), BF16), pltpu.VMEM((ROW_TILE, C_BR), BF16)],
        compiler_params=_cparams(("parallel", "arbitrary")),
        name="merge",
    )(ya, yb, yc, z, z, wa, wb, wc, z, z, z)


def _outproj_body(h_ref, m_ref, w_ref, o_ref):
    o_ref[...] = h_ref[...] + jnp.dot(m_ref[...], w_ref[...], preferred_element_type=F32)


def _outproj(h, m, w, tn=1024):
    n_rows = h.shape[0]
    return pl.pallas_call(
        _outproj_body,
        grid=(n_rows // ROW_TILE, D_MODEL // tn),
        in_specs=[
            pl.BlockSpec((ROW_TILE, tn), lambda i, j: (i, j)),
            pl.BlockSpec((ROW_TILE, D_MODEL), lambda i, j: (i, 0)),
            pl.BlockSpec((D_MODEL, tn), lambda i, j: (0, j)),
        ],
        out_specs=pl.BlockSpec((ROW_TILE, tn), lambda i, j: (i, j)),
        out_shape=jax.ShapeDtypeStruct((n_rows, D_MODEL), F32),
        compiler_params=_cparams(("parallel", "parallel")),
        name="outproj",
    )(h, m, w)


def _final_norm_body(x_ref, g_ref, o_ref):
    x = x_ref[...]
    y = x * lax.rsqrt(jnp.mean(x * x, axis=-1, keepdims=True) + EPS)
    o_ref[...] = y * g_ref[...]


def _final_norm(h, g, tm=512):
    return pl.pallas_call(
        _final_norm_body,
        grid=(SEQ // tm,),
        in_specs=[
            pl.BlockSpec((pl.Element(tm), pl.Element(D_MODEL)), lambda i: (pl.multiple_of(N_META + i * tm, N_META), 0)),
            pl.BlockSpec((1, D_MODEL), lambda i: (0, 0)),
        ],
        out_specs=pl.BlockSpec((tm, D_MODEL), lambda i: (i, 0)),
        out_shape=jax.ShapeDtypeStruct((SEQ, D_MODEL), F32),
        compiler_params=_cparams(("parallel",)),
        name="final_norm",
    )(h, g.reshape(1, D_MODEL))


def _hyena_branch(z, conv_w, conv_b, w1, b1, w2, b2, w3, b3, w4, freq, decay, skip):
    tab = _hyena_tables()
    fwd = [[_bf(tab["fwd_r"])], [_bf(tab["fwd_i"])]]
    inv_r, inv_i = _bf(tab["inv_r"]), _bf(tab["inv_i"])
    cb, sb = _bf(tab["cb"]), _bf(tab["sb"])

    hyc = _shortconv(z, conv_w, conv_b)
    filt = _hyena_filters(w1, b1, w2, b2, w3, b3, w4, freq, decay)
    ftr, fti = _stage_a([filt], [0], filt.shape[-1], fwd, BF16)
    kr, ki = _filter_spectrum(ftr, fti, cb, sb)

    hyc = _view3(hyc)
    tr, ti = _stage_a([hyc], [0], C_BR, fwd, BF16)
    tr, ti = _conv_stage_b(tr, ti, kr, ki, 0, cb, sb)
    zmid = _stage_ainv(tr, ti, inv_r, inv_i, hyc, 0, hyc, C_BR, skip[0], flat=False)

    tr, ti = _stage_a([zmid], [0], C_BR, fwd, BF16)
    tr, ti = _conv_stage_b(tr, ti, kr, ki, 1, cb, sb)
    return _stage_ainv(tr, ti, inv_r, inv_i, zmid, 0, hyc, 2 * C_BR, skip[1], flat=True)


def _fnet_branch(z):
    tab = _fnet_tables()
    cs, sn = _bf(tab["cs"]), _bf(tab["sn"])
    p, q = _chan_dft(z, OFF_FN_IN, _bf(tab["chan"]))
    tr, ti = _stage_a([p, q], [0, 0], C_BR, [[cs, -sn], [-sn, -cs]], BF16)
    return _fnet_stage_b(tr, ti, _bf(tab["cb"]), _bf(tab["sb"]))


def _layer(h, norm_g, w_in_bf16, conv_w, conv_b, w1, b1, w2, b2, w3, b3, w4, freq, decay, skip, rpb, meta_bias,
           w_a, w_b, w_c, w_out):
    z = _inproj(h, norm_g, w_in_bf16)
    ya = _hyena_branch(z, conv_w, conv_b, w1, b1, w2, b2, w3, b3, w4, freq, decay, skip)
    yb = _fnet_branch(z)
    yc = _neighborhood_attention(z, rpb, meta_bias)
    m = _merge(ya, yb, yc, w_a.astype(BF16), w_b.astype(BF16), w_c.astype(BF16), z)
    return _outproj(h, m, w_out.astype(BF16))


def kernel(x, meta_tokens, norm_g, w_in, hy_conv_w, hy_conv_b, hy_flt_w1, hy_flt_b1, hy_flt_w2, hy_flt_b2,
           hy_flt_w3, hy_flt_b3, hy_flt_w4, hy_flt_freq, hy_decay, hy_skip, na_rpb, na_meta_bias,
           w_branch_a, w_branch_b, w_branch_c, w_out, final_g):
    assert x.shape == (1, SEQ, D_MODEL)
    h = jnp.concatenate([meta_tokens.astype(x.dtype), x[0]], axis=0)
    for i in range(norm_g.shape[0]):
        h = _layer(h, norm_g[i], _to_bf16(w_in, i), hy_conv_w[i], hy_conv_b[i], hy_flt_w1[i], hy_flt_b1[i],
                   hy_flt_w2[i], hy_flt_b2[i], hy_flt_w3[i], hy_flt_b3[i], hy_flt_w4[i], hy_flt_freq[i],
                   hy_decay[i].reshape(-1), hy_skip[i], na_rpb[i], na_meta_bias[i],
                   w_branch_a[i], w_branch_b[i], w_branch_c[i], w_out[i])
    return _final_norm(h, final_g)[None]
```

```python
import functools
import math

import numpy as np
import jax
import jax.numpy as jnp
from jax import lax
from jax.experimental import pallas as pl
from jax.experimental.pallas import tpu as pltpu

F32 = jnp.float32
BF16 = jnp.bfloat16

D_MODEL = 2048
SEQ = 16384
N_META = 16
SEQ_T = SEQ + N_META
GRID_W = 64
GRID_H = SEQ // GRID_W
C_BR = 1024
HY_EMB = 33
HY_HID = 64
NA_HEADS = 16
NA_DH = 64
NA_KH = 8
NA_KW = 16
EPS = 1e-6
NEG_INF = -1e30

OFF_HY_IN = 0
OFF_HY_GATE = 3072
OFF_FN_IN = 4096
OFF_FN_GATE = 5120
OFF_Q = 6144
OFF_K = 7168
OFF_V = 8192
OFF_NA_GATE = 9216
OFF_MERGE = 10240
N_IN = 16384

FA = 80
FB = 205
FBP = 208
BBLK = 8
KA_HY = FA + 1
KAP_HY = 96

ROW_TILE = 656
LANE = 128
VMEM_LIMIT = 48 * 1024 * 1024


def _cparams(sem):
    return pltpu.CompilerParams(dimension_semantics=sem, vmem_limit_bytes=VMEM_LIMIT)


@functools.lru_cache(maxsize=None)
def _hyena_tables():
    n_circ = 2 * SEQ_T
    a = np.arange(FA)[None, None, :]
    b = np.arange(FBP)[:, None, None]
    ka = np.arange(KAP_HY)[None, :, None]
    n = FB * a + b
    ang = 2.0 * np.pi * ((ka * n) % n_circ) / n_circ
    valid = (b < FB) & (ka < KA_HY)
    fwd_r = np.where(valid, np.cos(ang), 0.0)
    fwd_i = np.where(valid, -np.sin(ang), 0.0)
    c = np.where((ka == 0) | (ka == FA), 1.0, 2.0) / n_circ
    inv_r = np.transpose(np.where(valid, c * np.cos(ang), 0.0), (0, 2, 1))
    inv_i = np.transpose(np.where(valid, -c * np.sin(ang), 0.0), (0, 2, 1))
    kb = np.arange(FBP)[:, None]
    bb = np.arange(FBP)[None, :]
    phi = 2.0 * np.pi * ((kb * bb) % FB) / FB
    ok = (kb < FB) & (bb < FB)
    cb = np.where(ok, np.cos(phi), 0.0)
    sb = np.where(ok, np.sin(phi), 0.0)
    return dict(fwd_r=fwd_r, fwd_i=fwd_i, inv_r=inv_r, inv_i=inv_i, cb=cb, sb=sb)


@functools.lru_cache(maxsize=None)
def _fnet_tables():
    a = np.arange(FA)[None, None, :]
    b = np.arange(FBP)[:, None, None]
    ka = np.arange(FA)[None, :, None]
    n = FB * a + b
    ang = 2.0 * np.pi * ((ka * n) % SEQ_T) / SEQ_T
    valid = np.broadcast_to(b < FB, ang.shape)
    cs = np.where(valid, np.cos(ang), 0.0)
    sn = np.where(valid, np.sin(ang), 0.0)
    kb = np.arange(FBP)[:, None]
    bb = np.arange(FBP)[None, :]
    phi = 2.0 * np.pi * ((kb * bb) % FB) / FB
    ok = (kb < FB) & (bb < FB)
    scale = 1.0 / math.sqrt(SEQ_T * 256.0)
    cb = np.where(ok, np.cos(phi), 0.0) * scale
    sb = np.where(ok, np.sin(phi), 0.0) * scale
    j = np.arange(256)
    th = 2.0 * np.pi * ((j[:, None] * j[None, :]) % 256) / 256.0
    chan = np.concatenate([np.cos(th), np.sin(th)], axis=1)
    return dict(cs=cs, sn=sn, cb=cb, sb=sb, chan=chan)


@functools.lru_cache(maxsize=None)
def _filter_features():
    t = np.linspace(0.0, 1.0, SEQ_T)[:, None]
    bands = (HY_EMB - 1) // 2
    w = 2.0 * np.pi * np.arange(SEQ_T)[:, None] / SEQ_T
    f = np.linspace(1e-4, bands - 1, bands)[None, :]
    z = np.concatenate([t, np.cos(f * w), -np.sin(f * w)], axis=-1)
    out = np.zeros((SEQ_T, HY_HID), np.float64)
    out[:, :HY_EMB] = z
    return out


@functools.lru_cache(maxsize=None)
def _na_index_tables():
    qc = np.arange(GRID_W)[:, None]
    kc = np.arange(GRID_W)[None, :]
    cs = np.clip(qc - NA_KW // 2, 0, GRID_W - NA_KW)
    colmask = (kc >= cs) & (kc < cs + NA_KW)
    dc = np.clip(kc - qc + NA_KW - 1, 0, 2 * NA_KW - 2)
    return colmask, dc


def _bf(x):
    return jnp.asarray(np.asarray(x, np.float32), dtype=BF16)


def _inproj_body(x_ref, g_ref, w_ref, o_ref, xn_ref):
    @pl.when(pl.program_id(1) == 0)
    def _():
        x = x_ref[...]
        y = x * lax.rsqrt(jnp.mean(x * x, axis=-1, keepdims=True) + EPS)
        xn_ref[...] = (y * g_ref[...]).astype(BF16)

    o_ref[...] = jnp.dot(xn_ref[...], w_ref[...], preferred_element_type=F32)


def _cast_body(x_ref, o_ref):
    o_ref[...] = x_ref[...].astype(o_ref.dtype)


def _to_bf16(w_stack, layer, tm=128):
    _, n_rows, n_cols = w_stack.shape
    return pl.pallas_call(
        _cast_body,
        grid=(n_rows // tm,),
        in_specs=[pl.BlockSpec((None, tm, n_cols), lambda i: (layer, i, 0))],
        out_specs=pl.BlockSpec((tm, n_cols), lambda i: (i, 0)),
        out_shape=jax.ShapeDtypeStruct((n_rows, n_cols), BF16),
        compiler_params=_cparams(("parallel",)),
        name="cast_bf16",
    )(w_stack)


def _inproj(h, g, w_bf16, tn=2048):
    n_rows, d = h.shape
    n_out = w_bf16.shape[1]
    return pl.pallas_call(
        _inproj_body,
        grid=(n_rows // ROW_TILE, n_out // tn),
        in_specs=[
            pl.BlockSpec((ROW_TILE, d), lambda i, j: (i, 0)),
            pl.BlockSpec((1, d), lambda i, j: (0, 0)),
            pl.BlockSpec((d, tn), lambda i, j: (0, j)),
        ],
        out_specs=pl.BlockSpec((ROW_TILE, tn), lambda i, j: (i, j)),
        out_shape=jax.ShapeDtypeStruct((n_rows, n_out), F32),
        scratch_shapes=[pltpu.VMEM((ROW_TILE, d), BF16)],
        compiler_params=_cparams(("parallel", "arbitrary")),
        name="inproj",
    )(h, g.reshape(1, d), w_bf16)


def _shortconv_body(prev_ref, cur_ref, next_ref, w_ref, b_ref, o_ref):
    i = pl.program_id(0)
    last = pl.num_programs(0) - 1
    x = cur_ref[...]
    rows = lax.broadcasted_iota(jnp.int32, x.shape, 0)
    prev_row = jnp.where(i == 0, 0.0, prev_ref[7:8, :])
    next_row = jnp.where(i == last, 0.0, next_ref[0:1, :])
    up = jnp.where(rows == 0, prev_row, pltpu.roll(x, 1, 0))
    dn = jnp.where(rows == ROW_TILE - 1, next_row, pltpu.roll(x, ROW_TILE - 1, 0))
    y = up * w_ref[0:1, :]
    y = y + x * w_ref[1:2, :]
    y = y + dn * w_ref[2:3, :]
    o_ref[...] = y + b_ref[...]


def _shortconv(z, w, b, tc=1024):
    n_rows = z.shape[0]
    n_c = w.shape[1]
    n8 = n_rows // 8
    r8 = ROW_TILE // 8
    return pl.pallas_call(
        _shortconv_body,
        grid=(n_rows // ROW_TILE, n_c // tc),
        in_specs=[
            pl.BlockSpec((8, tc), lambda i, j: (jnp.maximum(i * r8 - 1, 0), j)),
            pl.BlockSpec((ROW_TILE, tc), lambda i, j: (i, j)),
            pl.BlockSpec((8, tc), lambda i, j: (jnp.minimum((i + 1) * r8, n8 - 1), j)),
            pl.BlockSpec((3, tc), lambda i, j: (0, j)),
            pl.BlockSpec((1, tc), lambda i, j: (0, j)),
        ],
        out_specs=pl.BlockSpec((ROW_TILE, tc), lambda i, j: (i, j)),
        out_shape=jax.ShapeDtypeStruct((n_rows, n_c), F32),
        compiler_params=_cparams(("parallel", "parallel")),
        name="shortconv",
    )(z, z, z, w, b.reshape(1, n_c))


def _hp_dot(a, b):
    return jnp.dot(a, b, preferred_element_type=F32, precision=lax.Precision.HIGHEST)


def _split_bf16(x):
    hi = x.astype(BF16)
    return hi, (x - hi.astype(F32)).astype(BF16)


def _filter_body(z_ref, w1_ref, b1_ref, w2_ref, b2_ref, w3_ref, b3_ref, fr_ref, fr3_ref, w4_ref, dec_ref, o_ref,
                 h_ref):
    i = pl.program_id(0)
    j = pl.program_id(1)

    @pl.when(j == 0)
    def _():
        fr = fr_ref[...]
        h = jnp.sin(fr * (_hp_dot(z_ref[...], w1_ref[...]) + b1_ref[...]))
        h = jnp.sin(fr * (_hp_dot(h, w2_ref[...]) + b2_ref[...]))
        h3 = jnp.sin(fr3_ref[...] * (_hp_dot(h, w3_ref[...]) + b3_ref[...]))
        hi, lo = _split_bf16(h3)
        lane = lax.broadcasted_iota(jnp.int32, h3.shape, 1)
        h_ref[...] = jnp.where((lane >= HY_HID) & (lane < 2 * HY_HID), lo, hi)

    t = z_ref[:, 0:1]
    whi, wlo = _split_bf16(w4_ref[...])
    wcat = jnp.concatenate([whi, whi, wlo], axis=0)
    y = jnp.dot(h_ref[...], wcat, preferred_element_type=F32) * jnp.exp(-t * jnp.abs(dec_ref[...]))
    rows = lax.broadcasted_iota(jnp.int32, y.shape, 0)
    drop = jnp.logical_and(jnp.logical_and(i == 0, j % 2 == 1), rows == 0)
    o_ref[...] = jnp.where(drop, 0.0, y)[:FB, :]


def _hyena_filters(w1, b1, w2, b2, w3, b3, w4, freq, decay):
    zf = np.zeros((FA, FBP, HY_HID), np.float64)
    zf[:, :FB] = _filter_features().reshape(FA, FB, HY_HID)
    zfeat = jnp.asarray(zf, dtype=F32)
    w1p = jnp.zeros((HY_HID, HY_HID), F32).at[:HY_EMB].set(w1)
    n_c = w4.shape[1]
    tc = C_BR
    small = lambda i, j: (0, 0)
    return pl.pallas_call(
        _filter_body,
        grid=(FA, n_c // tc),
        in_specs=[
            pl.BlockSpec((None, FBP, HY_HID), lambda i, j: (i, 0, 0)),
            pl.BlockSpec((HY_HID, HY_HID), small),
            pl.BlockSpec((1, HY_HID), small),
            pl.BlockSpec((HY_HID, HY_HID), small),
            pl.BlockSpec((1, HY_HID), small),
            pl.BlockSpec((HY_HID, 3 * HY_HID), small),
            pl.BlockSpec((1, 3 * HY_HID), small),
            pl.BlockSpec((1, HY_HID), small),
            pl.BlockSpec((1, 3 * HY_HID), small),
            pl.BlockSpec((HY_HID, tc), lambda i, j: (0, j)),
            pl.BlockSpec((1, tc), lambda i, j: (0, j)),
        ],
        out_specs=pl.BlockSpec((None, FB, tc), lambda i, j: (i, 0, j)),
        out_shape=jax.ShapeDtypeStruct((FA, FB, n_c), F32),
        scratch_shapes=[pltpu.VMEM((FBP, 3 * HY_HID), BF16)],
        compiler_params=_cparams(("parallel", "arbitrary")),
        name="hyena_filter",
    )(zfeat, w1p, b1.reshape(1, -1), w2, b2.reshape(1, -1), jnp.tile(w3, (1, 3)), jnp.tile(b3.reshape(1, -1), (1, 3)),
      freq.reshape(1, -1), jnp.tile(freq.reshape(1, -1), (1, 3)), w4, decay.reshape(1, n_c))


def _view3(x):
    return x if x.ndim == 3 else x.reshape(FA, FB, x.shape[1])


def _stage_a_body(n_in, n_out, *refs):
    x_refs = refs[:n_in]
    w_refs = refs[n_in:n_in + n_in * n_out]
    o_refs = refs[n_in + n_in * n_out:]
    bb = pl.program_id(0)
    for j in range(BBLK):
        valid = bb * BBLK + j < FB
        xs = [jnp.where(valid, xr[:, j, :], 0.0).astype(BF16) for xr in x_refs]
        for o in range(n_out):
            acc = None
            for k in range(n_in):
                d = jnp.dot(w_refs[o * n_in + k][j], xs[k], preferred_element_type=F32)
                acc = d if acc is None else acc + d
            o_refs[o][j] = acc.astype(o_refs[o].dtype)


def _stage_a(xs, col_offs, n_cols, w_tabs, out_dtype, ct=1024):
    n_in = len(xs)
    n_out = len(w_tabs)
    m = w_tabs[0][0].shape[1]
    x3 = [_view3(x) for x in xs]
    in_specs = []
    for k in range(n_in):
        off = col_offs[k] // ct
        in_specs.append(pl.BlockSpec((FA, BBLK, ct), lambda b, c, off=off: (0, b, off + c)))
    flat_w = []
    for o in range(n_out):
        for k in range(n_in):
            flat_w.append(w_tabs[o][k])
            in_specs.append(pl.BlockSpec((BBLK, m, FA), lambda b, c: (b, 0, 0)))
    return pl.pallas_call(
        functools.partial(_stage_a_body, n_in, n_out),
        grid=(FBP // BBLK, n_cols // ct),
        in_specs=in_specs,
        out_specs=[pl.BlockSpec((BBLK, m, ct), lambda b, c: (b, 0, c)) for _ in range(n_out)],
        out_shape=[jax.ShapeDtypeStruct((FBP, m, n_cols), out_dtype) for _ in range(n_out)],
        compiler_params=_cparams(("parallel", "parallel")),
        name="dft_stage_a",
    )(*x3, *flat_w)


def _stage_ainv_body(tr_ref, ti_ref, wr_ref, wi_ref, v_ref, x_ref, skip_ref, o_ref):
    skip = skip_ref[...]
    for j in range(BBLK):
        y = jnp.dot(wr_ref[j], tr_ref[j].astype(BF16), preferred_element_type=F32)
        y = y + jnp.dot(wi_ref[j], ti_ref[j].astype(BF16), preferred_element_type=F32)
        y = y + v_ref[:, j, :] * skip
        o_ref[:, j, :] = x_ref[:, j, :] * y


def _stage_ainv(tr, ti, wr, wi, v, v_off, xmul, x_off, skip, flat, ct=1024):
    n_cols = tr.shape[2]
    kap = tr.shape[1]
    view = _view3
    spec3 = lambda off: pl.BlockSpec((FA, BBLK, ct), lambda b, c, off=off // ct: (0, b, off + c))
    in_specs = [
        pl.BlockSpec((BBLK, kap, ct), lambda b, c: (b, 0, c)),
        pl.BlockSpec((BBLK, kap, ct), lambda b, c: (b, 0, c)),
        pl.BlockSpec((BBLK, FA, kap), lambda b, c: (b, 0, 0)),
        pl.BlockSpec((BBLK, FA, kap), lambda b, c: (b, 0, 0)),
        spec3(v_off),
        spec3(x_off),
        pl.BlockSpec((1, ct), lambda b, c: (0, c)),
    ]
    args = [tr, ti, wr, wi, view(v), view(xmul), skip.reshape(1, n_cols)]
    out = pl.pallas_call(
        _stage_ainv_body,
        grid=(FBP // BBLK, n_cols // ct),
        in_specs=in_specs,
        out_specs=pl.BlockSpec((FA, BBLK, ct), lambda b, c: (0, b, c)),
        out_shape=jax.ShapeDtypeStruct((FA, FB, n_cols), F32),
        compiler_params=_cparams(("parallel", "parallel")),
        name="dft_stage_a_inv",
    )(*args)
    return out.reshape(SEQ_T, n_cols) if flat else out


def _filter_spec_body(fr_ref, fi_ref, br_ref, bi_ref, c_ref, s_ref, kr_ref, ki_ref):
    c = c_ref[...]
    s = s_ref[...]
    dot = lambda a, b: jnp.dot(a, b[...], preferred_element_type=F32)
    kr_ref[...] = dot(c, fr_ref) + dot(s, fi_ref) + dot(c, br_ref) + dot(s, bi_ref)
    ki_ref[...] = dot(c, fi_ref) - dot(s, fr_ref) - dot(c, bi_ref) + dot(s, br_ref)


def _filter_spectrum(tr, ti, cb, sb, ct=1024):
    nc = C_BR // ct
    kap = tr.shape[1]
    n_cols = tr.shape[2]
    tr2 = tr.reshape(FBP, kap * n_cols)
    ti2 = ti.reshape(FBP, kap * n_cols)
    ncol_blk = n_cols // ct

    def tspec(direction):
        return pl.BlockSpec((FBP, ct), lambda o, ka, c: (0, ka * ncol_blk + (2 * o + direction) * nc + c))

    mat = pl.BlockSpec((FBP, FBP), lambda o, ka, c: (0, 0))
    ospec = pl.BlockSpec((None, None, FBP, ct), lambda o, ka, c: (o, ka, 0, c))
    oshape = jax.ShapeDtypeStruct((2, KA_HY, FBP, C_BR), F32)
    return pl.pallas_call(
        _filter_spec_body,
        grid=(2, KA_HY, nc),
        in_specs=[tspec(0), tspec(0), tspec(1), tspec(1), mat, mat],
        out_specs=[ospec, ospec],
        out_shape=[oshape, oshape],
        compiler_params=_cparams(("parallel", "parallel", "parallel")),
        name="filter_spectrum",
    )(tr2, ti2, tr2, ti2, cb, sb)


def _conv_b_body(tr_ref, ti_ref, kr_ref, ki_ref, c_ref, s_ref, or_ref, oi_ref):
    ka = pl.program_id(0)

    @pl.when(ka < KA_HY)
    def _():
        c = c_ref[...]
        s = s_ref[...]
        dot = lambda a, b: jnp.dot(a, b, preferred_element_type=F32)
        tr = tr_ref[...]
        ti = ti_ref[...]
        yr = dot(c, tr) + dot(s, ti)
        yi = dot(c, ti) - dot(s, tr)
        kr = kr_ref[...]
        ki = ki_ref[...]
        zr = (yr * kr - yi * ki).astype(BF16)
        zi = (yr * ki + yi * kr).astype(BF16)
        or_ref[...] = (dot(c, zr) - dot(s, zi)).astype(BF16)
        oi_ref[...] = (dot(c, zi) + dot(s, zr)).astype(BF16)

    @pl.when(ka >= KA_HY)
    def _():
        or_ref[...] = jnp.zeros_like(or_ref)
        oi_ref[...] = jnp.zeros_like(oi_ref)


def _conv_stage_b(tr, ti, kr, ki, order, cb, sb, ct=1024):
    kap = tr.shape[1]
    n_cols = tr.shape[2]
    nc = n_cols // ct
    tr2 = tr.reshape(FBP, kap * n_cols)
    ti2 = ti.reshape(FBP, kap * n_cols)
    tspec = pl.BlockSpec((FBP, ct), lambda ka, c: (0, ka * nc + c))
    kspec = pl.BlockSpec((None, None, FBP, ct), lambda ka, c: (order, jnp.minimum(ka, KA_HY - 1), 0, c))
    mat = pl.BlockSpec((FBP, FBP), lambda ka, c: (0, 0))
    oshape = jax.ShapeDtypeStruct((FBP, kap * n_cols), BF16)
    o_r, o_i = pl.pallas_call(
        _conv_b_body,
        grid=(kap, nc),
        in_specs=[tspec, tspec, kspec, kspec, mat, mat],
        out_specs=[tspec, tspec],
        out_shape=[oshape, oshape],
        compiler_params=_cparams(("parallel", "parallel")),
        name="conv_stage_b",
    )(tr2, ti2, kr, ki, cb, sb)
    return o_r.reshape(FBP, kap, n_cols), o_i.reshape(FBP, kap, n_cols)


def _fnet_b_body(tr_ref, ti_ref, c_ref, s_ref, o_ref):
    dot = lambda a, b: jnp.dot(a, b, preferred_element_type=F32)
    y = dot(c_ref[...], tr_ref[...]) + dot(s_ref[...], ti_ref[...])
    o_ref[...] = y[:FB, :]


def _fnet_stage_b(tr, ti, cb, sb, ct=1024):
    n_cols = tr.shape[2]
    nc = n_cols // ct
    tr2 = tr.reshape(FBP, FA * n_cols)
    ti2 = ti.reshape(FBP, FA * n_cols)
    tspec = pl.BlockSpec((FBP, ct), lambda ka, c: (0, ka * nc + c))
    mat = pl.BlockSpec((FBP, FBP), lambda ka, c: (0, 0))
    out = pl.pallas_call(
        _fnet_b_body,
        grid=(FA, nc),
        in_specs=[tspec, tspec, mat, mat],
        out_specs=pl.BlockSpec((FB, ct), lambda ka, c: (0, ka * nc + c)),
        out_shape=jax.ShapeDtypeStruct((FB, FA * n_cols), F32),
        compiler_params=_cparams(("parallel", "parallel")),
        name="fnet_stage_b",
    )(tr2, ti2, cb, sb)
    return out.reshape(SEQ_T, n_cols)


def _chan_dft_body(x_ref, w_ref, p_ref, q_ref):
    y = jnp.dot(x_ref[...].astype(BF16), w_ref[...], preferred_element_type=F32)
    p_ref[...] = y[:, :256]
    q_ref[...] = y[:, 256:]


def _chan_dft(z, off, chan):
    oshape = jax.ShapeDtypeStruct((SEQ_T, C_BR), F32)
    return pl.pallas_call(
        _chan_dft_body,
        grid=(SEQ_T // ROW_TILE, 4),
        in_specs=[
            pl.BlockSpec((ROW_TILE, 256), lambda i, g: (i, off // 256 + g)),
            pl.BlockSpec((256, 512), lambda i, g: (0, 0)),
        ],
        out_specs=[pl.BlockSpec((ROW_TILE, 256), lambda i, g: (i, g))] * 2,
        out_shape=[oshape, oshape],
        compiler_params=_cparams(("parallel", "parallel")),
        name="fnet_chan_dft",
    )(z, chan)


NA_QROWS = 8
NA_SLAB = 16
NA_SCALE = NA_DH ** -0.5


def _na_main_body(q_ref, k_ref, v_ref, km_ref, vm_ref, bt_ref, mb_ref, g_ref, o_ref):
    rb = pl.program_id(1)
    slab0 = jnp.clip(rb * NA_QROWS - NA_KH // 2, 0, GRID_H - NA_SLAB)
    w2 = 2 * NA_DH
    lane = lax.broadcasted_iota(jnp.int32, (GRID_W, w2), 1)
    row2 = lax.broadcasted_iota(jnp.int32, (w2, w2), 0)
    lane2 = lax.broadcasted_iota(jnp.int32, (w2, w2), 1)
    own_head = (row2 >= NA_DH) == (lane2 >= NA_DH)
    km = km_ref[...].astype(BF16)
    vm = vm_ref[...].astype(BF16)
    mbt = mb_ref[...]
    dn_t = (((1,), (1,)), ((), ()))
    dn_k = (((0,), (0,)), ((), ()))
    for i in range(NA_QROWS):
        r = rb * NA_QROWS + i
        r0 = jnp.clip(r - NA_KH // 2, 0, GRID_H - NA_KH)
        off = pl.multiple_of((r0 - slab0) * GRID_W, GRID_W)
        d0 = r0 - r + (NA_KH - 1)
        q = q_ref[i * GRID_W:(i + 1) * GRID_W, :]
        q2 = jnp.where(own_head, jnp.concatenate([q, q], axis=0), 0.0).astype(BF16)
        ks = k_ref[pl.ds(off, NA_KH * GRID_W), :].astype(BF16)
        vs = v_ref[pl.ds(off, NA_KH * GRID_W), :].astype(BF16)
        st = lax.dot_general(ks, q2, dn_t, preferred_element_type=F32) * NA_SCALE + bt_ref[d0]
        sx = lax.dot_general(km, q2, dn_t, preferred_element_type=F32) * NA_SCALE + mbt
        m = jnp.maximum(jnp.max(st, axis=0, keepdims=True), jnp.max(sx, axis=0, keepdims=True))
        p = jnp.exp(st - m)
        px = jnp.exp(sx - m)
        inv = 1.0 / (jnp.sum(p, axis=0, keepdims=True) + jnp.sum(px, axis=0, keepdims=True))
        o2 = lax.dot_general((p * inv).astype(BF16), vs, dn_k, preferred_element_type=F32)
        o2 = o2 + lax.dot_general((px * inv).astype(BF16), vm, dn_k, preferred_element_type=F32)
        o = jnp.where(lane < NA_DH, o2[:GRID_W], o2[GRID_W:])
        g = g_ref[i * GRID_W:(i + 1) * GRID_W, :]
        o_ref[i * GRID_W:(i + 1) * GRID_W, :] = (o * (g * jax.nn.sigmoid(g))).astype(BF16)


def _na_meta_body(q_ref, k_ref, v_ref, mb_ref, g_ref, y_ref, o_ref):
    del y_ref
    lane = lax.broadcasted_iota(jnp.int32, (N_META, 2 * NA_DH), 1)
    q = q_ref[...]
    km = k_ref[...].astype(BF16)
    vm = v_ref[...].astype(BF16)
    dn_t = (((1,), (1,)), ((), ()))
    outs = []
    for hh in range(2):
        sel = (lane >= hh * NA_DH) & (lane < (hh + 1) * NA_DH)
        qm = jnp.where(sel, q, 0.0).astype(BF16)
        s = lax.dot_general(qm, km, dn_t, preferred_element_type=F32) * NA_SCALE + mb_ref[hh, 0:1, :]
        m = jnp.max(s, axis=-1, keepdims=True)
        p = jnp.exp(s - m)
        den = jnp.sum(p, axis=-1, keepdims=True)
        outs.append(jnp.dot(p.astype(BF16), vm, preferred_element_type=F32) / den)
    o = jnp.where(lane < NA_DH, outs[0], outs[1])
    g = g_ref[...]
    o_ref[...] = (o * (g * jax.nn.sigmoid(g))).astype(BF16)


def _na_bias_table(rpb):
    colmask, dc = _na_index_tables()
    rows = np.arange(NA_KH)[:, None] + np.arange(NA_KH)[None, :]
    t = rpb.astype(F32)[:, rows]
    t = t[:, :, :, dc]
    t = jnp.where(colmask[None, None, None], t, NEG_INF)
    t = t.reshape(NA_HEADS // 2, 2, NA_KH, NA_KH, GRID_W, GRID_W)
    t = jnp.transpose(t, (0, 2, 3, 5, 1, 4))
    return t.reshape(NA_HEADS // 2, NA_KH, NA_KH * GRID_W, 2 * GRID_W)


def _neighborhood_attention(z, rpb, meta_bias):
    bt = _na_bias_table(rpb)
    mb = jnp.broadcast_to(meta_bias.astype(F32).reshape(NA_HEADS // 2, 2, 1, N_META),
                          (NA_HEADS // 2, 2, 8, N_META))
    mbt = jnp.transpose(meta_bias.astype(F32).reshape(NA_HEADS // 2, 2, N_META), (0, 2, 1))
    mbt = jnp.repeat(mbt, GRID_W, axis=2)
    w2 = 2 * NA_DH
    qrows = NA_QROWS * GRID_W
    srows = NA_SLAB * GRID_W

    def slab_start(rb):
        start = N_META + GRID_W * jnp.clip(rb * NA_QROWS - NA_KH // 2, 0, GRID_H - NA_SLAB)
        return pl.multiple_of(start, N_META)

    def col(off):
        return lambda hp, rb: (pl.multiple_of(N_META + rb * qrows, N_META), pl.multiple_of(off + hp * w2, w2))

    def slab(off):
        return lambda hp, rb: (slab_start(rb), pl.multiple_of(off + hp * w2, w2))

    def meta(off):
        return lambda hp, rb: (0, off // w2 + hp)

    y_main = pl.pallas_call(
        _na_main_body,
        grid=(NA_HEADS // 2, GRID_H // NA_QROWS),
        in_specs=[
            pl.BlockSpec((pl.Element(qrows), pl.Element(w2)), col(OFF_Q)),
            pl.BlockSpec((pl.Element(srows), pl.Element(w2)), slab(OFF_K)),
            pl.BlockSpec((pl.Element(srows), pl.Element(w2)), slab(OFF_V)),
            pl.BlockSpec((N_META, w2), meta(OFF_K)),
            pl.BlockSpec((N_META, w2), meta(OFF_V)),
            pl.BlockSpec((None, NA_KH, NA_KH * GRID_W, w2), lambda hp, rb: (hp, 0, 0, 0)),
            pl.BlockSpec((None, N_META, w2), lambda hp, rb: (hp, 0, 0)),
            pl.BlockSpec((pl.Element(qrows), pl.Element(w2)), col(OFF_NA_GATE)),
        ],
        out_specs=pl.BlockSpec((pl.Element(qrows), pl.Element(w2)),
                               lambda hp, rb: (pl.multiple_of(N_META + rb * qrows, N_META),
                                               pl.multiple_of(hp * w2, w2))),
        out_shape=jax.ShapeDtypeStruct((SEQ_T, C_BR), BF16),
        compiler_params=_cparams(("parallel", "parallel")),
        name="na_main",
    )(z, z, z, z, z, bt, mbt, z)

    def mcol(off):
        return lambda hp: (0, off // w2 + hp)

    return pl.pallas_call(
        _na_meta_body,
        grid=(NA_HEADS // 2,),
        in_specs=[
            pl.BlockSpec((N_META, w2), mcol(OFF_Q)),
            pl.BlockSpec((N_META, w2), mcol(OFF_K)),
            pl.BlockSpec((N_META, w2), mcol(OFF_V)),
            pl.BlockSpec((None, 2, 8, N_META), lambda hp: (hp, 0, 0, 0)),
            pl.BlockSpec((N_META, w2), mcol(OFF_NA_GATE)),
            pl.BlockSpec((N_META, w2), lambda hp: (0, hp)),
        ],
        out_specs=pl.BlockSpec((N_META, w2), lambda hp: (0, hp)),
        out_shape=jax.ShapeDtypeStruct((SEQ_T, C_BR), BF16),
        input_output_aliases={5: 0},
        compiler_params=_cparams(("parallel",)),
        name="na_meta",
    )(z, z, z, mb, z, y_main)


def _silu(g):
    return g * jax.nn.sigmoid(g)


def _merge_body(ya_ref, yb_ref, yc_ref, hg_ref, fg_ref, wa_ref, wb_ref, wc_ref, ga_ref, gb_ref, gc_ref, o_ref,
                sa_ref, sb_ref):
    @pl.when(pl.program_id(1) == 0)
    def _():
        sa_ref[...] = (ya_ref[...] * _silu(hg_ref[...])).astype(BF16)
        sb_ref[...] = (yb_ref[...] * _silu(fg_ref[...])).astype(BF16)

    dot = lambda a, b: jnp.dot(a[...], b[...], preferred_element_type=F32)
    m = jax.nn.sigmoid(ga_ref[...]) * dot(sa_ref, wa_ref)
    m = m + jax.nn.sigmoid(gb_ref[...]) * dot(sb_ref, wb_ref)
    m = m + jax.nn.sigmoid(gc_ref[...]) * dot(yc_ref, wc_ref)
    o_ref[...] = m.astype(BF16)


def _merge(ya, yb, yc, wa, wb, wc, z, tn=512):
    n_rows = ya.shape[0]
    yspec = pl.BlockSpec((ROW_TILE, C_BR), lambda i, j: (i, 0))
    wspec = pl.BlockSpec((C_BR, tn), lambda i, j: (0, j))
    zspec = lambda off: pl.BlockSpec((ROW_TILE, C_BR), lambda i, j, off=off: (i, off // C_BR))
    gspec = lambda k: pl.BlockSpec((ROW_TILE, tn), lambda i, j, k=k: (i, (OFF_MERGE + k * D_MODEL) // tn + j))
    return pl.pallas_call(
        _merge_body,
        grid=(n_rows // ROW_TILE, D_MODEL // tn),
        in_specs=[yspec, yspec, yspec, zspec(OFF_HY_GATE), zspec(OFF_FN_GATE), wspec, wspec, wspec,
                  gspec(0), gspec(1), gspec(2)],
        out_specs=pl.BlockSpec((ROW_TILE, tn), lambda i, j: (i, j)),
        out_shape=jax.ShapeDtypeStruct((n_rows, D_MODEL), BF16),
        scratch_shapes=[pltpu.VMEM((ROW_TILE, C_BR), BF16), pltpu.VMEM((ROW_TILE, C_BR), BF16)],
        compiler_params=_cparams(("parallel", "arbitrary")),
        name="merge",
    )(ya, yb, yc, z, z, wa, wb, wc, z, z, z)


def _outproj_body(h_ref, m_ref, w_ref, o_ref):
    o_ref[...] = h_ref[...] + jnp.dot(m_ref[...], w_ref[...], preferred_element_type=F32)


def _outproj(h, m, w, tn=1024):
    n_rows = h.shape[0]
    return pl.pallas_call(
        _outproj_body,
        grid=(n_rows // ROW_TILE, D_MODEL // tn),
        in_specs=[
            pl.BlockSpec((ROW_TILE, tn), lambda i, j: (i, j)),
            pl.BlockSpec((ROW_TILE, D_MODEL), lambda i, j: (i, 0)),
            pl.BlockSpec((D_MODEL, tn), lambda i, j: (0, j)),
        ],
        out_specs=pl.BlockSpec((ROW_TILE, tn), lambda i, j: (i, j)),
        out_shape=jax.ShapeDtypeStruct((n_rows, D_MODEL), F32),
        compiler_params=_cparams(("parallel", "parallel")),
        name="outproj",
    )(h, m, w)


def _final_norm_body(x_ref, g_ref, o_ref):
    x = x_ref[...]
    y = x * lax.rsqrt(jnp.mean(x * x, axis=-1, keepdims=True) + EPS)
    o_ref[...] = y * g_ref[...]


def _final_norm(h, g, tm=512):
    return pl.pallas_call(
        _final_norm_body,
        grid=(SEQ // tm,),
        in_specs=[
            pl.BlockSpec((pl.Element(tm), pl.Element(D_MODEL)), lambda i: (pl.multiple_of(N_META + i * tm, N_META), 0)),
            pl.BlockSpec((1, D_MODEL), lambda i: (0, 0)),
        ],
        out_specs=pl.BlockSpec((tm, D_MODEL), lambda i: (i, 0)),
        out_shape=jax.ShapeDtypeStruct((SEQ, D_MODEL), F32),
        compiler_params=_cparams(("parallel",)),
        name="final_norm",
    )(h, g.reshape(1, D_MODEL))


def _hyena_branch(z, conv_w, conv_b, w1, b1, w2, b2, w3, b3, w4, freq, decay, skip):
    tab = _hyena_tables()
    fwd = [[_bf(tab["fwd_r"])], [_bf(tab["fwd_i"])]]
    inv_r, inv_i = _bf(tab["inv_r"]), _bf(tab["inv_i"])
    cb, sb = _bf(tab["cb"]), _bf(tab["sb"])

    hyc = _shortconv(z, conv_w, conv_b)
    filt = _hyena_filters(w1, b1, w2, b2, w3, b3, w4, freq, decay)
    ftr, fti = _stage_a([filt], [0], filt.shape[-1], fwd, BF16)
    kr, ki = _filter_spectrum(ftr, fti, cb, sb)

    hyc = _view3(hyc)
    tr, ti = _stage_a([hyc], [0], C_BR, fwd, BF16)
    tr, ti = _conv_stage_b(tr, ti, kr, ki, 0, cb, sb)
    zmid = _stage_ainv(tr, ti, inv_r, inv_i, hyc, 0, hyc, C_BR, skip[0], flat=False)

    tr, ti = _stage_a([zmid], [0], C_BR, fwd, BF16)
    tr, ti = _conv_stage_b(tr, ti, kr, ki, 1, cb, sb)
    return _stage_ainv(tr, ti, inv_r, inv_i, zmid, 0, hyc, 2 * C_BR, skip[1], flat=True)


def _fnet_branch(z):
    tab = _fnet_tables()
    cs, sn = _bf(tab["cs"]), _bf(tab["sn"])
    p, q = _chan_dft(z, OFF_FN_IN, _bf(tab["chan"]))
    tr, ti = _stage_a([p, q], [0, 0], C_BR, [[cs, -sn], [-sn, -cs]], BF16)
    return _fnet_stage_b(tr, ti, _bf(tab["cb"]), _bf(tab["sb"]))


def _layer(h, norm_g, w_in_bf16, conv_w, conv_b, w1, b1, w2, b2, w3, b3, w4, freq, decay, skip, rpb, meta_bias,
           w_a, w_b, w_c, w_out):
    z = _inproj(h, norm_g, w_in_bf16)
    ya = _hyena_branch(z, conv_w, conv_b, w1, b1, w2, b2, w3, b3, w4, freq, decay, skip)
    yb = _fnet_branch(z)
    yc = _neighborhood_attention(z, rpb, meta_bias)
    m = _merge(ya, yb, yc, w_a.astype(BF16), w_b.astype(BF16), w_c.astype(BF16), z)
    return _outproj(h, m, w_out.astype(BF16))


def kernel(x, meta_tokens, norm_g, w_in, hy_conv_w, hy_conv_b, hy_flt_w1, hy_flt_b1, hy_flt_w2, hy_flt_b2,
           hy_flt_w3, hy_flt_b3, hy_flt_w4, hy_flt_freq, hy_decay, hy_skip, na_rpb, na_meta_bias,
           w_branch_a, w_branch_b, w_branch_c, w_out, final_g):
    assert x.shape == (1, SEQ, D_MODEL)
    h = jnp.concatenate([meta_tokens.astype(x.dtype), x[0]], axis=0)
    for i in range(norm_g.shape[0]):
        h = _layer(h, norm_g[i], _to_bf16(w_in, i), hy_conv_w[i], hy_conv_b[i], hy_flt_w1[i], hy_flt_b1[i],
                   hy_flt_w2[i], hy_flt_b2[i], hy_flt_w3[i], hy_flt_b3[i], hy_flt_w4[i], hy_flt_freq[i],
                   hy_decay[i].reshape(-1), hy_skip[i], na_rpb[i], na_meta_bias[i],
                   w_branch_a[i], w_branch_b[i], w_branch_c[i], w_out[i])
    return _final_norm(h, final_g)[None]
```
